```python
import jax, jax.numpy as jnp
from jax import lax
import numpy as np

D_MODEL = 1024
BATCH = 2
SEQ = 8192
DEPTH = 4
DEC_BATCH = 128
DEC_SEQ = 8
PAST_LEN = 2048
PAGE_SIZE = 128

N_MIXERS = 3
N_HEADS = 16
HEAD_DIM = D_MODEL // N_HEADS
D_FF = 128 * ((8 * D_MODEL // 3 + 127) // 128)
CF_WIDTH = 31
SC_WIDTH = 3
FFN_WIDTH = 3
Q_BLOCK = 128
SB_BIAS_INIT = -7.0
RMS_EPS = 1e-6
LN_EPS = 1e-5
N_CF = (DEPTH + 2) // 3
N_SC = (DEPTH + 1) // 3
N_SB = DEPTH // 3

kernel_name = 'hybrid_conformer_shortconv_stickbreaking_decoder_step'


def _rms(x, g):
    xf = x.astype(jnp.float32)
    y = xf * lax.rsqrt(jnp.mean(xf * xf, axis=-1, keepdims=True) + RMS_EPS)
    return (y * g.astype(jnp.float32)).astype(x.dtype)


def _ln(x, g, b):
    xf = x.astype(jnp.float32)
    xc = xf - jnp.mean(xf, axis=-1, keepdims=True)
    y = xc * lax.rsqrt(jnp.mean(xc * xc, axis=-1, keepdims=True) + LN_EPS)
    return (y * g.astype(jnp.float32) + b.astype(jnp.float32)).astype(x.dtype)


def _causal_dwconv(buf, u, w):
    width = w.shape[0]
    xp = jnp.concatenate([buf.astype(u.dtype), u], axis=1)
    y = lax.conv_general_dilated(xp, w[:, None, :].astype(u.dtype), window_strides=(1,),
                                 padding='VALID', dimension_numbers=('NWC', 'WIO', 'NWC'),
                                 feature_group_count=u.shape[-1])
    return y, xp[:, xp.shape[1] - (width - 1):]


def _conformer_conv(h, buf, w1, b1, w_dw, b_dw, ln_g, ln_b, w2, b2):
    a, g = jnp.split(h @ w1 + b1, 2, axis=-1)
    u = a * jax.nn.sigmoid(g)
    y, new_buf = _causal_dwconv(buf, u, w_dw)
    y = jax.nn.silu(_ln(y + b_dw, ln_g, ln_b))
    return y @ w2 + b2, new_buf


def _short_conv(h, buf, w_in, w_conv, w_out):
    b_gate, c_gate, xv = jnp.split(h @ w_in, 3, axis=-1)
    y, new_buf = _causal_dwconv(buf, c_gate * xv, w_conv)
    return (b_gate * y) @ w_out, new_buf


def _conv_ffn(h, buf, w_up, w_conv, b_conv, w_down):
    y, new_buf = _causal_dwconv(buf, h @ w_up, w_conv)
    a, g = jnp.split(y + b_conv, 2, axis=-1)
    return (jax.nn.silu(g) * a) @ w_down, new_buf


def _sb_core(q, k, v, bias, q_pos, k_pos):
    z = jnp.einsum('nqhd,nkhd->nhqk', q, k, preferred_element_type=jnp.float32) * (HEAD_DIM ** -0.5)
    z = z + bias.astype(jnp.float32)[None, :, None, None]
    mask = k_pos[None, :] < q_pos[:, None]
    log_1m = jnp.where(mask, jax.nn.log_sigmoid(-z), 0.0)
    later = lax.cumsum(log_1m, axis=3, reverse=True) - log_1m
    a = jnp.where(mask, jnp.exp(jax.nn.log_sigmoid(z) + later), 0.0)
    return jnp.einsum('nhqk,nkhd->nqhd', a.astype(v.dtype), v)


def _sb_prompt(q, k, v, bias):
    n, t = q.shape[0], q.shape[1]
    nb = t // Q_BLOCK
    qb = q.reshape(n, nb, Q_BLOCK, N_HEADS, HEAD_DIM).transpose(1, 0, 2, 3, 4)
    k_pos = jnp.arange(t)

    def block(args):
        qi, bi = args
        return _sb_core(qi, k, v, bias, bi * Q_BLOCK + jnp.arange(Q_BLOCK), k_pos)

    o = lax.map(block, (qb, jnp.arange(nb)))
    return o.transpose(1, 0, 2, 3, 4).reshape(n, t, N_HEADS, HEAD_DIM)


def _trunk(x, c, cf_state, sc_state, ffn_state, attend, p):
    cf_new, sc_new, ffn_new, k_new, v_new = [], [], [], [], []
    c_act = jax.nn.silu(c)
    for i in range(DEPTH):
        kind, j = i % N_MIXERS, i // N_MIXERS
        mod = (c_act @ p['w_mod'][i] + p['b_mod'][i])[:, None, :]
        sh1, sc1, g1, sh2, sc2, g2 = jnp.split(mod, 6, axis=-1)
        h = _rms(x, p['g_pre_mix'][i]) * (1.0 + sc1) + sh1
        if kind == 0:
            out, nbuf = _conformer_conv(h, cf_state[j], p['cf_w1'][j], p['cf_b1'][j], p['cf_w_dw'][j],
                                        p['cf_b_dw'][j], p['cf_ln_g'][j], p['cf_ln_b'][j],
                                        p['cf_w2'][j], p['cf_b2'][j])
            cf_new.append(nbuf)
        elif kind == 1:
            out, nbuf = _short_conv(h, sc_state[j], p['sc_w_in'][j], p['sc_w_conv'][j], p['sc_w_out'][j])
            sc_new.append(nbuf)
        else:
            n, t = h.shape[0], h.shape[1]
            q, k, v = [u.reshape(n, t, N_HEADS, HEAD_DIM)
                       for u in jnp.split(h @ p['sb_w_qkv'][j], 3, axis=-1)]
            out = attend(j, q, k, v, p['sb_bias'][j]).reshape(n, t, D_MODEL) @ p['sb_w_o'][j]
            k_new.append(k)
            v_new.append(v)
        x = x + g1 * _rms(out, p['g_post_mix'][i])
        h = _rms(x, p['g_pre_ffn'][i]) * (1.0 + sc2) + sh2
        out, nbuf = _conv_ffn(h, ffn_state[i], p['ffn_w_up'][i], p['ffn_w_conv'][i],
                              p['ffn_b_conv'][i], p['ffn_w_down'][i])
        ffn_new.append(nbuf)
        x = x + g2 * _rms(out, p['g_post_ffn'][i])
    return (x, jnp.stack(cf_new), jnp.stack(sc_new), jnp.stack(ffn_new),
            jnp.stack(k_new), jnp.stack(v_new))


def setup_inputs(seed: int = 0) -> dict:
    key = jax.random.key(seed)
    keys = jax.random.split(key, 64)
    counter = [0]

    def nrm(shape, scale=1.0):
        kk = keys[counter[0]]
        counter[0] += 1
        return jax.random.normal(kk, shape, jnp.float32) * scale

    d, f = D_MODEL, D_FF
    n_pages = PAST_LEN // PAGE_SIZE
    n_used = DEC_BATCH * n_pages
    n_pool = n_used + max(1, n_used // 4)
    inp = {}
    inp['x_prompt'] = nrm((BATCH, SEQ, d))
    inp['x_sample'] = nrm((DEC_BATCH, DEC_SEQ, d))
    inp['c_prompt'] = nrm((BATCH, d))
    inp['c_sample'] = nrm((DEC_BATCH, d))
    inp['state_cf_conv'] = nrm((N_CF, DEC_BATCH, CF_WIDTH - 1, d), 0.5)
    inp['state_sc_conv'] = nrm((N_SC, DEC_BATCH, SC_WIDTH - 1, d), 0.5)
    inp['state_ffn_conv'] = nrm((DEPTH, DEC_BATCH, FFN_WIDTH - 1, 2 * f))
    inp['cache_k'] = nrm((N_SB, n_pool, PAGE_SIZE, N_HEADS, HEAD_DIM))
    inp['cache_v'] = nrm((N_SB, n_pool, PAGE_SIZE, N_HEADS, HEAD_DIM))
    inp['page_table'] = jax.random.permutation(keys[63], n_pool)[:n_used].reshape(
        DEC_BATCH, n_pages).astype(jnp.int32)
    inp['g_pre_mix'] = 1.0 + nrm((DEPTH, d), 0.02)
    inp['g_post_mix'] = 1.0 + nrm((DEPTH, d), 0.02)
    inp['g_pre_ffn'] = 1.0 + nrm((DEPTH, d), 0.02)
    inp['g_post_ffn'] = 1.0 + nrm((DEPTH, d), 0.02)
    inp['w_mod'] = nrm((DEPTH, d, 6 * d), d ** -0.5)
    inp['b_mod'] = nrm((DEPTH, 6 * d), 0.02)
    inp['ffn_w_up'] = nrm((DEPTH, d, 2 * f), d ** -0.5)
    inp['ffn_w_conv'] = nrm((DEPTH, FFN_WIDTH, 2 * f), FFN_WIDTH ** -0.5)
    inp['ffn_b_conv'] = nrm((DEPTH, 2 * f), 0.02)
    inp['ffn_w_down'] = nrm((DEPTH, f, d), f ** -0.5)
    inp['cf_w1'] = nrm((N_CF, d, 2 * d), d ** -0.5)
    inp['cf_b1'] = nrm((N_CF, 2 * d), 0.02)
    inp['cf_w_dw'] = nrm((N_CF, CF_WIDTH, d), CF_WIDTH ** -0.5)
    inp['cf_b_dw'] = nrm((N_CF, d), 0.02)
    inp['cf_ln_g'] = 1.0 + nrm((N_CF, d), 0.02)
    inp['cf_ln_b'] = nrm((N_CF, d), 0.02)
    inp['cf_w2'] = nrm((N_CF, d, d), d ** -0.5)
    inp['cf_b2'] = nrm((N_CF, d), 0.02)
    inp['sc_w_in'] = nrm((N_SC, d, 3 * d), d ** -0.5)
    inp['sc_w_conv'] = nrm((N_SC, SC_WIDTH, d), SC_WIDTH ** -0.5)
    inp['sc_w_out'] = nrm((N_SC, d, d), d ** -0.5)
    inp['sb_w_qkv'] = nrm((N_SB, d, 3 * d), d ** -0.5)
    inp['sb_bias'] = SB_BIAS_INIT + nrm((N_SB, N_HEADS), 0.1)
    inp['sb_w_o'] = nrm((N_SB, d, d), d ** -0.5)
    return inp


def reference(x_prompt, x_sample, c_prompt, c_sample, state_cf_conv, state_sc_conv, state_ffn_conv,
              cache_k, cache_v, page_table, g_pre_mix, g_post_mix, g_pre_ffn, g_post_ffn, w_mod, b_mod,
              ffn_w_up, ffn_w_conv, ffn_b_conv, ffn_w_down, cf_w1, cf_b1, cf_w_dw, cf_b_dw, cf_ln_g,
              cf_ln_b, cf_w2, cf_b2, sc_w_in, sc_w_conv, sc_w_out, sb_w_qkv, sb_bias, sb_w_o):
    p = dict(g_pre_mix=g_pre_mix, g_post_mix=g_post_mix, g_pre_ffn=g_pre_ffn, g_post_ffn=g_post_ffn,
             w_mod=w_mod, b_mod=b_mod, ffn_w_up=ffn_w_up, ffn_w_conv=ffn_w_conv, ffn_b_conv=ffn_b_conv,
             ffn_w_down=ffn_w_down, cf_w1=cf_w1, cf_b1=cf_b1, cf_w_dw=cf_w_dw, cf_b_dw=cf_b_dw,
             cf_ln_g=cf_ln_g, cf_ln_b=cf_ln_b, cf_w2=cf_w2, cf_b2=cf_b2, sc_w_in=sc_w_in,
             sc_w_conv=sc_w_conv, sc_w_out=sc_w_out, sb_w_qkv=sb_w_qkv, sb_bias=sb_bias, sb_w_o=sb_w_o)

    b, s = x_prompt.shape[0], x_prompt.shape[1]

    def fresh(st):
        return jnp.zeros((st.shape[0], b) + st.shape[2:], x_prompt.dtype)

    y_p, cf_p, sc_p, ffn_p, k_p, v_p = _trunk(
        x_prompt, c_prompt, fresh(state_cf_conv), fresh(state_sc_conv), fresh(state_ffn_conv),
        lambda j, q, k, v, bias: _sb_prompt(q, k, v, bias), p)

    n_dec, n_pages = page_table.shape
    past = n_pages * PAGE_SIZE

    def attend_sample(j, q, k, v, bias):
        pk = cache_k[j][page_table].reshape(n_dec, past, N_HEADS, HEAD_DIM).astype(k.dtype)
        pv = cache_v[j][page_table].reshape(n_dec, past, N_HEADS, HEAD_DIM).astype(v.dtype)
        kk = jnp.concatenate([pk, k], axis=1)
        vv = jnp.concatenate([pv, v], axis=1)
        t = q.shape[1]
        return _sb_core(q, kk, vv, bias, past + jnp.arange(t), jnp.arange(past + t))

    y_s, cf_s, sc_s, ffn_s, k_s, v_s = _trunk(
        x_sample, c_sample, state_cf_conv, state_sc_conv, state_ffn_conv, attend_sample, p)

    page_shape = (k_p.shape[0], b, s // PAGE_SIZE, PAGE_SIZE, N_HEADS, HEAD_DIM)
    return (y_p, y_s, k_p.reshape(page_shape), v_p.reshape(page_shape), k_s, v_s,
            cf_p, cf_s, sc_p, sc_s, ffn_p, ffn_s)
```

```python
import functools
import math

import jax
import jax.numpy as jnp
from jax import lax
from jax.experimental import pallas as pl
from jax.experimental.pallas import tpu as pltpu

F32 = jnp.float32
BF16 = jnp.bfloat16

D_MODEL = 1024
N_HEADS = 16
HEAD_DIM = 64
PAGE_SIZE = 128
RMS_EPS = 1e-6
LN_EPS = 1e-5
LOG2E = 1.4426950408889634

SUBLANES = 8
LANES = 128
VMEM_LIMIT = 56 * 1024 * 1024

TM_PROMPT = 512
TM_SAMPLE = 256
TQ = 256
HEAD_PAIR = 2 * HEAD_DIM


def _cparams(sem):
    return pltpu.CompilerParams(dimension_semantics=sem, vmem_limit_bytes=VMEM_LIMIT)


def _const_spec(shape):
    nd = len(shape)
    return pl.BlockSpec(shape, lambda *_: (0,) * nd, pipeline_mode=pl.Buffered(1))


def _rms(x, g):
    return x * lax.rsqrt(jnp.mean(x * x, axis=-1, keepdims=True) + RMS_EPS) * g


def _ln(x, g, b):
    xc = x - jnp.mean(x, axis=-1, keepdims=True)
    return xc * lax.rsqrt(jnp.mean(xc * xc, axis=-1, keepdims=True) + LN_EPS) * g + b


def _sigmoid(x):
    return 1.0 / (1.0 + jnp.exp(-x))


def _mm(a, w):
    return jnp.dot(a.astype(BF16), w, preferred_element_type=F32)


def _conv_prompt(ext_ref, u, w_ref, width, first):
    tm = u.shape[0]
    halo = ext_ref.shape[0] - tm

    @pl.when(first)
    def _():
        ext_ref[0:halo, :] = jnp.zeros((halo, ext_ref.shape[1]), F32)

    ext_ref[halo:halo + tm, :] = u
    y = None
    for k in range(width):
        off = halo - (width - 1) + k
        term = w_ref[k:k + 1, :] * ext_ref[off:off + tm, :]
        y = term if y is None else y + term
    tail = ext_ref[tm:tm + halo, :]
    ext_ref[0:halo, :] = tail
    return y, tail


def _conv_sample(hist_ref, u, wj_ref, width):
    tb = hist_ref.shape[0]
    c = u.shape[1]
    u3 = u.reshape(tb, SUBLANES, c)
    y = None
    for j in range(width - 1 + SUBLANES):
        if j < width - 1:
            row = hist_ref[:, j:j + 1, :]
        else:
            row = u3[:, j - (width - 1):j - (width - 1) + 1, :]
        term = wj_ref[j][None, :, :] * row
        y = term if y is None else y + term
    return y.reshape(tb * SUBLANES, c)


def _premod(x, g_ref, sh, sc):
    return _rms(x, g_ref[...]) * (1.0 + sc) + sh


def _postmod(x, out, g_ref, gate):
    return x + gate * _rms(out, g_ref[...])


def _modvals(refs, sample):
    return [r[...] if sample else r[0] for r in refs]


def _mod_kernel(c_ref, w_ref, b_ref, o_ref):
    c = c_ref[...]
    act = c * _sigmoid(c)
    o_ref[0] = _mm(act, w_ref[0].astype(BF16)) + b_ref[0]


def _modulation(c_all, w_mod, b_mod):
    depth, d, n = w_mod.shape
    rows = c_all.shape[0]
    tn = 1536
    return pl.pallas_call(
        _mod_kernel,
        grid=(depth, n // tn),
        in_specs=[
            pl.BlockSpec((rows, d), lambda i, j: (0, 0)),
            pl.BlockSpec((1, d, tn), lambda i, j: (i, 0, j)),
            pl.BlockSpec((1, 1, tn), lambda i, j: (i, 0, j)),
        ],
        out_specs=pl.BlockSpec((1, rows, tn), lambda i, j: (i, 0, j)),
        out_shape=jax.ShapeDtypeStruct((depth, rows, n), F32),
        compiler_params=_cparams(("arbitrary", "arbitrary")),
        name="adaln_modulation",
    )(c_all, w_mod, b_mod.reshape(depth, 1, n))


def _row_call(kernel, sample, x, mods, consts, extra_in, outs, scratch, name):
    d = x.shape[-1]
    if sample:
        tm = TM_SAMPLE
        grid = (x.shape[0] // tm,)
        x_spec = pl.BlockSpec((tm, d), lambda i: (i, 0))
        mod_specs = [pl.BlockSpec((tm, d), lambda i: (i, 0)) for _ in mods]
        sem = ("arbitrary",)
    else:
        tm = TM_PROMPT
        grid = (x.shape[0], x.shape[1] // tm)
        x_spec = pl.BlockSpec((1, tm, d), lambda b, i: (b, i, 0))
        mod_specs = [pl.BlockSpec((1, 1, d), lambda b, i: (b, 0, 0)) for _ in mods]
        sem = ("arbitrary", "arbitrary")
    in_specs = [x_spec] + mod_specs + [_const_spec(c.shape) for c in consts] + [s for _, s in extra_in]
    args = [x] + list(mods) + list(consts) + [a for a, _ in extra_in]
    out_shape = [jax.ShapeDtypeStruct(s, dt) for s, dt, _ in outs]
    out_specs = [sp for _, _, sp in outs]
    return pl.pallas_call(
        kernel,
        grid=grid,
        in_specs=in_specs,
        out_specs=out_specs,
        out_shape=out_shape,
        scratch_shapes=scratch,
        compiler_params=_cparams(sem),
        name=name,
    )(*args)


def _x_out(x, sample):
    d = x.shape[-1]
    if sample:
        return (x.shape, F32, pl.BlockSpec((TM_SAMPLE, d), lambda i: (i, 0)))
    return (x.shape, F32, pl.BlockSpec((1, TM_PROMPT, d), lambda b, i: (b, i, 0)))


def _tail_out(batch, halo, c):
    return ((batch, halo, c), F32, pl.BlockSpec((1, halo, c), lambda b, i: (b, 0, 0)))


def _rows_out(rows, c):
    return ((rows, c), F32, pl.BlockSpec((TM_SAMPLE, c), lambda i: (i, 0)))


def _hist_in(state):
    _, w1, c = state.shape
    return (state, pl.BlockSpec((TM_SAMPLE // SUBLANES, w1, c), lambda i: (i, 0, 0)))


def _toeplitz(w):
    width, c = w.shape
    j = jnp.arange(width - 1 + SUBLANES)[:, None]
    t = jnp.arange(SUBLANES)[None, :]
    k = j - t
    valid = (k >= 0) & (k < width)
    return jnp.where(valid[:, :, None], w[jnp.clip(k, 0, width - 1)], 0.0)


CF_WIDTH = 31
CF_HALO = 32


def _cf_kernel(sample, x_ref, sh_ref, sc_ref, gt_ref, gpre_ref, gpost_ref, w1_ref, b1_ref, wdw_ref,
               bdw_ref, lng_ref, lnb_ref, w2_ref, b2_ref, *rest):
    if sample:
        hist_ref, xo_ref, u_ref = rest
    else:
        xo_ref, u_ref, ext_ref = rest
    sh, sc, gt = _modvals((sh_ref, sc_ref, gt_ref), sample)
    x = x_ref[...] if sample else x_ref[0]
    d = x.shape[1]
    h = _premod(x, gpre_ref, sh, sc)
    ag = _mm(h, w1_ref[...]) + b1_ref[...]
    u = ag[:, :d] * _sigmoid(ag[:, d:])
    if sample:
        y = _conv_sample(hist_ref, u, wdw_ref, CF_WIDTH)
        u_ref[...] = u
    else:
        y, tail = _conv_prompt(ext_ref, u, wdw_ref, CF_WIDTH, pl.program_id(1) == 0)
        u_ref[0] = tail
    y = _ln(y + bdw_ref[...], lng_ref[...], lnb_ref[...])
    y = y * _sigmoid(y)
    out = _mm(y, w2_ref[...]) + b2_ref[...]
    xn = _postmod(x, out, gpost_ref, gt)
    if sample:
        xo_ref[...] = xn
    else:
        xo_ref[0] = xn


def _cf_layer(sample, x, mods, gpre, gpost, w1, b1, wdw, bdw, lng, lnb, w2, b2, state=None):
    d = x.shape[-1]
    consts = [gpre, gpost, w1, b1, _toeplitz(wdw) if sample else wdw, bdw, lng, lnb, w2, b2]
    if sample:
        outs = [_x_out(x, True), _rows_out(x.shape[0], d)]
        extra, scratch = [_hist_in(state)], []
    else:
        outs = [_x_out(x, False), _tail_out(x.shape[0], CF_HALO, d)]
        extra, scratch = [], [pltpu.VMEM((CF_HALO + TM_PROMPT, d), F32)]
    return _row_call(functools.partial(_cf_kernel, sample), sample, x, mods, consts, extra, outs, scratch,
                     "conformer_mixer_sample" if sample else "conformer_mixer_prompt")


SC_WIDTH = 3
SHORT_HALO = 8


def _sc_kernel(sample, x_ref, sh_ref, sc_ref, gt_ref, gpre_ref, gpost_ref, win_ref, wcv_ref, wout_ref, *rest):
    if sample:
        hist_ref, xo_ref, u_ref = rest
    else:
        xo_ref, u_ref, ext_ref = rest
    sh, sc, gt = _modvals((sh_ref, sc_ref, gt_ref), sample)
    x = x_ref[...] if sample else x_ref[0]
    d = x.shape[1]
    h = _premod(x, gpre_ref, sh, sc)
    bcx = _mm(h, win_ref[...])
    cx = bcx[:, d:2 * d] * bcx[:, 2 * d:]
    if sample:
        y = _conv_sample(hist_ref, cx, wcv_ref, SC_WIDTH)
        u_ref[...] = cx
    else:
        y, tail = _conv_prompt(ext_ref, cx, wcv_ref, SC_WIDTH, pl.program_id(1) == 0)
        u_ref[0] = tail
    out = _mm(bcx[:, :d] * y, wout_ref[...])
    xn = _postmod(x, out, gpost_ref, gt)
    if sample:
        xo_ref[...] = xn
    else:
        xo_ref[0] = xn


def _sc_layer(sample, x, mods, gpre, gpost, win, wcv, wout, state=None):
    d = x.shape[-1]
    consts = [gpre, gpost, win, _toeplitz(wcv) if sample else wcv, wout]
    if sample:
        outs = [_x_out(x, True), _rows_out(x.shape[0], d)]
        extra, scratch = [_hist_in(state)], []
    else:
        outs = [_x_out(x, False), _tail_out(x.shape[0], SHORT_HALO, d)]
        extra, scratch = [], [pltpu.VMEM((SHORT_HALO + TM_PROMPT, d), F32)]
    return _row_call(functools.partial(_sc_kernel, sample), sample, x, mods, consts, extra, outs, scratch,
                     "shortconv_mixer_sample" if sample else "shortconv_mixer_prompt")


FFN_WIDTH = 3
FFN_CHUNK = 1408


def _ffn_kernel(sample, x_ref, sh_ref, sc_ref, gt_ref, gpre_ref, gpost_ref, wup_ref, wcv_ref, bcv_ref,
                wdn_ref, *rest):
    if sample:
        hist_ref, xo_ref, up_ref = rest
    else:
        xo_ref, up_ref, ext_ref, prev_ref = rest
    sh, sc, gt = _modvals((sh_ref, sc_ref, gt_ref), sample)
    x = x_ref[...] if sample else x_ref[0]
    tm = x.shape[0]
    f = wdn_ref.shape[0]
    h = _premod(x, gpre_ref, sh, sc).astype(BF16)
    out = None
    for c0 in range(0, f, FFN_CHUNK):
        halves = []
        for base in (c0, f + c0):
            cols = slice(base, base + FFN_CHUNK)
            up = jnp.dot(h, wup_ref[:, cols], preferred_element_type=F32)
            if sample:
                tb = tm // SUBLANES
                up3 = up.reshape(tb, SUBLANES, FFN_CHUNK)
                y = None
                for j in range(FFN_WIDTH - 1 + SUBLANES):
                    if j < FFN_WIDTH - 1:
                        row = hist_ref[:, j:j + 1, cols]
                    else:
                        row = up3[:, j - (FFN_WIDTH - 1):j - (FFN_WIDTH - 1) + 1, :]
                    term = wcv_ref[j, :, cols][None, :, :] * row
                    y = term if y is None else y + term
                y = y.reshape(tm, FFN_CHUNK)
                up_ref[:, cols] = up
            else:
                first = pl.program_id(1) == 0

                @pl.when(first)
                def _():
                    prev_ref[:, cols] = jnp.zeros((SHORT_HALO, FFN_CHUNK), F32)

                ext_ref[0:SHORT_HALO, :] = prev_ref[:, cols]
                ext_ref[SHORT_HALO:SHORT_HALO + tm, :] = up
                y = None
                for k in range(FFN_WIDTH):
                    off = SHORT_HALO - (FFN_WIDTH - 1) + k
                    term = wcv_ref[k:k + 1, cols] * ext_ref[off:off + tm, :]
                    y = term if y is None else y + term
                tail = up[tm - SHORT_HALO:, :]
                prev_ref[:, cols] = tail
                up_ref[0, :, cols] = tail
            halves.append(y + bcv_ref[:, cols])
        a, g = halves
        act = (g * _sigmoid(g)) * a
        part = _mm(act, wdn_ref[c0:c0 + FFN_CHUNK, :])
        out = part if out is None else out + part
    xn = _postmod(x, out, gpost_ref, gt)
    if sample:
        xo_ref[...] = xn
    else:
        xo_ref[0] = xn


def _ffn_layer(sample, x, mods, gpre, gpost, wup, wcv, bcv, wdn, state=None):
    c = wup.shape[1]
    consts = [gpre, gpost, wup, _toeplitz(wcv) if sample else wcv, bcv, wdn]
    if sample:
        outs = [_x_out(x, True), _rows_out(x.shape[0], c)]
        extra, scratch = [_hist_in(state)], []
    else:
        outs = [_x_out(x, False), _tail_out(x.shape[0], SHORT_HALO, c)]
        extra = []
        scratch = [pltpu.VMEM((SHORT_HALO + TM_PROMPT, FFN_CHUNK), F32), pltpu.VMEM((SHORT_HALO, c), F32)]
    return _row_call(functools.partial(_ffn_kernel, sample), sample, x, mods, consts, extra, outs, scratch,
                     "convffn_sample" if sample else "convffn_prompt")


def _qkv_kernel(sample, x_ref, sh_ref, sc_ref, gpre_ref, w_ref, q_ref, k_ref, v_ref):
    sh, sc = _modvals((sh_ref, sc_ref), sample)
    x = x_ref[...] if sample else x_ref[0]
    d = x.shape[1]
    h = _premod(x, gpre_ref, sh, sc)
    qkv = _mm(h, w_ref[...])
    for idx, ref in enumerate((q_ref, k_ref, v_ref)):
        val = qkv[:, idx * d:(idx + 1) * d]
        if sample:
            ref[...] = val
        else:
            ref[0] = val


def _qkv_layer(sample, x, mods, gpre, w):
    outs = [_x_out(x, sample)] * 3
    return _row_call(functools.partial(_qkv_kernel, sample), sample, x, mods, [gpre, w], [], outs, [],
                     "qkv_proj_sample" if sample else "qkv_proj_prompt")


def _attn_out_kernel(sample, x_ref, gt_ref, gpost_ref, w_ref, o_ref, xo_ref):
    (gt,) = _modvals((gt_ref,), sample)
    x = x_ref[...] if sample else x_ref[0]
    o = o_ref[...] if sample else o_ref[0]
    xn = _postmod(x, _mm(o, w_ref[...]), gpost_ref, gt)
    if sample:
        xo_ref[...] = xn
    else:
        xo_ref[0] = xn


def _attn_out_layer(sample, x, mods, gpost, w, o):
    _, _, spec = _x_out(x, sample)
    return _row_call(functools.partial(_attn_out_kernel, sample), sample, x, mods, [gpost, w], [(o, spec)],
                     [_x_out(x, sample)], [], "attn_out_sample" if sample else "attn_out_prompt")[0]


def _softplus2(z):
    return jnp.maximum(z, 0.0) + jnp.log2(1.0 + jnp.exp2(-jnp.abs(z)))


def _split_bf16(x):
    hi = x.astype(BF16)
    lo = (x - hi.astype(F32)).astype(BF16)
    return hi, lo


def _sb_prompt_kernel(bias_ref, q_ref, k_ref, v_ref, tri_ref, o_ref, kb_ref, vt_ref, qt_ref, acc_ref, carry_ref):
    p = pl.program_id(1)
    i = pl.program_id(2)
    nblk = kb_ref.shape[0]

    @pl.when(i == 0)
    def _():
        def cast(j, carry):
            rows = pl.ds(pl.multiple_of(j * TQ, TQ), TQ)
            kb_ref[j] = k_ref[0, rows, :].astype(BF16)
            vt_ref[j] = v_ref[0, rows, :].T.astype(BF16)
            return carry
        lax.fori_loop(0, nblk, cast, 0)

    lane = lax.broadcasted_iota(jnp.int32, (1, HEAD_PAIR), 1)
    q = q_ref[0] * (HEAD_DIM ** -0.5 * LOG2E)
    q_even = jnp.where(lane < HEAD_DIM, q, 0.0).T.astype(BF16)
    q_odd = jnp.where(lane >= HEAD_DIM, q, 0.0).T.astype(BF16)
    qt_ref[:, 0:TQ] = q_even
    qt_ref[:, TQ:2 * TQ] = q_odd
    col = lax.broadcasted_iota(jnp.int32, (1, 2 * TQ), 1)
    bias = jnp.where(col < TQ, bias_ref[2 * p], bias_ref[2 * p + 1]) * LOG2E
    acc_ref[...] = jnp.zeros_like(acc_ref)
    carry_ref[...] = jnp.zeros_like(carry_ref)
    tri = tri_ref[...]

    def step(j, masked):
        z = jnp.dot(kb_ref[j], qt_ref[...], preferred_element_type=F32) + bias
        sp = _softplus2(z)
        if masked:
            kpos = lax.broadcasted_iota(jnp.int32, (TQ, 2 * TQ), 0)
            qpos = lax.broadcasted_iota(jnp.int32, (TQ, 2 * TQ), 1)
            qpos = jnp.where(qpos >= TQ, qpos - TQ, qpos)
            mask = kpos < qpos
            spm = jnp.where(mask, sp, 0.0)
        else:
            spm = sp
        hi, lo = _split_bf16(spm)
        later = (jnp.dot(tri, hi, preferred_element_type=F32)
                 + jnp.dot(tri, lo, preferred_element_type=F32))
        a = jnp.exp2(z - sp - later - carry_ref[...])
        if masked:
            a = jnp.where(mask, a, 0.0)
        acc_ref[...] += jnp.dot(vt_ref[j], a.astype(BF16), preferred_element_type=F32)
        carry_ref[...] += jnp.sum(spm, axis=0, keepdims=True)

    step(i, True)

    def body(jj, carry):
        step(i - 1 - jj, False)
        return carry

    lax.fori_loop(0, i, body, 0)

    acc = acc_ref[...]
    row = lax.broadcasted_iota(jnp.int32, (HEAD_PAIR, 1), 0)
    out_t = jnp.where(row < HEAD_DIM, acc[:, 0:TQ], acc[:, TQ:2 * TQ])
    o_ref[0] = out_t.T


def _sb_prompt(q, k, v, bias):
    b, s, d = q.shape
    nblk = s // TQ
    tri = (jnp.arange(TQ)[None, :] > jnp.arange(TQ)[:, None]).astype(BF16)
    grid_spec = pltpu.PrefetchScalarGridSpec(
        num_scalar_prefetch=1,
        grid=(b, d // HEAD_PAIR, nblk),
        in_specs=[
            pl.BlockSpec((1, TQ, HEAD_PAIR), lambda bi, p, i, *_: (bi, i, p)),
            pl.BlockSpec((1, s, HEAD_PAIR), lambda bi, p, i, *_: (bi, 0, p)),
            pl.BlockSpec((1, s, HEAD_PAIR), lambda bi, p, i, *_: (bi, 0, p)),
            pl.BlockSpec((TQ, TQ), lambda bi, p, i, *_: (0, 0)),
        ],
        out_specs=pl.BlockSpec((1, TQ, HEAD_PAIR), lambda bi, p, i, *_: (bi, i, p)),
        scratch_shapes=[
            pltpu.VMEM((nblk, TQ, HEAD_PAIR), BF16),
            pltpu.VMEM((nblk, HEAD_PAIR, TQ), BF16),
            pltpu.VMEM((HEAD_PAIR, 2 * TQ), BF16),
            pltpu.VMEM((HEAD_PAIR, 2 * TQ), F32),
            pltpu.VMEM((1, 2 * TQ), F32),
        ],
    )
    return pl.pallas_call(
        _sb_prompt_kernel,
        grid_spec=grid_spec,
        out_shape=jax.ShapeDtypeStruct((b, s, d), F32),
        compiler_params=_cparams(("arbitrary", "arbitrary", "arbitrary")),
        name="stickbreaking_prompt",
    )(bias, q, k, v, tri)


def _sb_sample_kernel(pt_ref, q_ref, kn_ref, vn_ref, kc_ref, vc_ref, bias_ref, tri_ref, o_ref,
                      qe_ref, acc_ref, carry_ref):
    s = pl.program_id(1)
    nsteps = pl.num_programs(1)
    d = q_ref.shape[2]
    rows = N_HEADS * SUBLANES

    def visit(kb, vb, masked):
        z = lax.dot_general(qe_ref[...], kb, (((1,), (1,)), ((), ())), preferred_element_type=F32)
        z = z + bias_ref[...]
        sp = _softplus2(z)
        if masked:
            t = lax.broadcasted_iota(jnp.int32, (rows, PAGE_SIZE), 0) % SUBLANES
            kpos = lax.broadcasted_iota(jnp.int32, (rows, PAGE_SIZE), 1)
            mask = kpos < t
            spm = jnp.where(mask, sp, 0.0)
        else:
            spm = sp
        hi, lo = _split_bf16(spm)
        tri = tri_ref[...]
        later = (jnp.dot(hi, tri, preferred_element_type=F32)
                 + jnp.dot(lo, tri, preferred_element_type=F32))
        a = jnp.exp2(z - sp - later - carry_ref[...])
        if masked:
            a = jnp.where(mask, a, 0.0)
        acc_ref[...] += jnp.dot(a.astype(BF16), vb, preferred_element_type=F32)
        carry_ref[...] += jnp.sum(spm, axis=1, keepdims=True)

    @pl.when(s == 0)
    def _():
        q = q_ref[0] * (HEAD_DIM ** -0.5 * LOG2E)
        qrep = jnp.concatenate([q] * N_HEADS, axis=0)
        head_of_row = lax.broadcasted_iota(jnp.int32, (rows, d), 0) // SUBLANES
        head_of_col = lax.broadcasted_iota(jnp.int32, (rows, d), 1) // HEAD_DIM
        qe_ref[...] = jnp.where(head_of_row == head_of_col, qrep, 0.0).astype(BF16)
        acc_ref[...] = jnp.zeros_like(acc_ref)
        carry_ref[...] = jnp.zeros_like(carry_ref)
        pad = jnp.zeros((PAGE_SIZE - SUBLANES, d), F32)
        kb = jnp.concatenate([kn_ref[0], pad], axis=0).astype(BF16)
        vb = jnp.concatenate([vn_ref[0], pad], axis=0).astype(BF16)
        visit(kb, vb, True)

    @pl.when(s > 0)
    def _():
        visit(kc_ref[0].astype(BF16), vc_ref[0].astype(BF16), False)

    @pl.when(s == nsteps - 1)
    def _():
        acc = acc_ref[...]
        head_of_col = lax.broadcasted_iota(jnp.int32, (SUBLANES, d), 1) // HEAD_DIM
        out = jnp.zeros((SUBLANES, d), F32)
        for hh in range(N_HEADS):
            out = out + jnp.where(head_of_col == hh, acc[hh * SUBLANES:(hh + 1) * SUBLANES, :], 0.0)
        o_ref[0] = out


def _sb_sample(q, k, v, cache_k, cache_v, page_table, bias):
    n, t, d = q.shape
    n_pages = page_table.shape[1]
    rows = N_HEADS * SUBLANES
    bias_rows = jnp.broadcast_to(jnp.repeat(bias * LOG2E, SUBLANES)[:, None], (rows, PAGE_SIZE)).astype(F32)
    tri = (jnp.arange(PAGE_SIZE)[:, None] > jnp.arange(PAGE_SIZE)[None, :]).astype(BF16)

    def page_map(ni, si, pt):
        pg = jnp.clip(n_pages - si, 0, n_pages - 1)
        return (pt[ni * n_pages + pg], 0, 0)

    grid_spec = pltpu.PrefetchScalarGridSpec(
        num_scalar_prefetch=1,
        grid=(n, n_pages + 1),
        in_specs=[
            pl.BlockSpec((1, t, d), lambda ni, si, pt: (ni, 0, 0)),
            pl.BlockSpec((1, t, d), lambda ni, si, pt: (ni, 0, 0)),
            pl.BlockSpec((1, t, d), lambda ni, si, pt: (ni, 0, 0)),
            pl.BlockSpec((1, PAGE_SIZE, d), page_map),
            pl.BlockSpec((1, PAGE_SIZE, d), page_map),
            pl.BlockSpec((rows, PAGE_SIZE), lambda ni, si, pt: (0, 0)),
            pl.BlockSpec((PAGE_SIZE, PAGE_SIZE), lambda ni, si, pt: (0, 0)),
        ],
        out_specs=pl.BlockSpec((1, t, d), lambda ni, si, pt: (ni, 0, 0)),
        scratch_shapes=[
            pltpu.VMEM((rows, d), BF16),
            pltpu.VMEM((rows, d), F32),
            pltpu.VMEM((rows, 1), F32),
        ],
    )
    return pl.pallas_call(
        _sb_sample_kernel,
        grid_spec=grid_spec,
        out_shape=jax.ShapeDtypeStruct((n, t, d), F32),
        compiler_params=_cparams(("arbitrary", "arbitrary")),
        name="stickbreaking_sample",
    )(page_table.reshape(-1), q, k, v, cache_k, cache_v, bias_rows, tri)


def _trunk(sample, x, mod, p, cf_state, sc_state, ffn_state, attend):
    depth = mod.shape[0]
    cf_new, sc_new, ffn_new, k_new, v_new = [], [], [], [], []
    for i in range(depth):
        kind, j = i % 3, i // 3
        if sample:
            m = [jnp.repeat(mod[i, :, c, :], SUBLANES, axis=0) for c in range(6)]
        else:
            m = [mod[i, :, c:c + 1, :] for c in range(6)]
        sh1, sc1, g1, sh2, sc2, g2 = m
        if kind == 0:
            x, nbuf = _cf_layer(sample, x, (sh1, sc1, g1), p['g_pre_mix'][i], p['g_post_mix'][i],
                                p['cf_w1'][j], p['cf_b1'][j], p['cf_w_dw'][j], p['cf_b_dw'][j],
                                p['cf_ln_g'][j], p['cf_ln_b'][j], p['cf_w2'][j], p['cf_b2'][j],
                                None if not sample else cf_state[j])
            cf_new.append(nbuf)
        elif kind == 1:
            x, nbuf = _sc_layer(sample, x, (sh1, sc1, g1), p['g_pre_mix'][i], p['g_post_mix'][i],
                                p['sc_w_in'][j], p['sc_w_conv'][j], p['sc_w_out'][j],
                                None if not sample else sc_state[j])
            sc_new.append(nbuf)
        else:
            q, k, v = _qkv_layer(sample, x, (sh1, sc1), p['g_pre_mix'][i], p['sb_w_qkv'][j])
            o = attend(j, q, k, v)
            x = _attn_out_layer(sample, x, (g1,), p['g_post_mix'][i], p['sb_w_o'][j], o)
            k_new.append(k)
            v_new.append(v)
        x, nbuf = _ffn_layer(sample, x, (sh2, sc2, g2), p['g_pre_ffn'][i], p['g_post_ffn'][i],
                             p['ffn_w_up'][i], p['ffn_w_conv'][i], p['ffn_b_conv'][i], p['ffn_w_down'][i],
                             None if not sample else ffn_state[i])
        ffn_new.append(nbuf)
    return x, cf_new, sc_new, ffn_new, k_new, v_new


def kernel(x_prompt, x_sample, c_prompt, c_sample, state_cf_conv, state_sc_conv, state_ffn_conv, cache_k, cache_v, page_table, g_pre_mix, g_post_mix, g_pre_ffn, g_post_ffn, w_mod, b_mod, ffn_w_up, ffn_w_conv, ffn_b_conv, ffn_w_down, cf_w1, cf_b1, cf_w_dw, cf_b_dw, cf_ln_g, cf_ln_b, cf_w2, cf_b2, sc_w_in, sc_w_conv, sc_w_out, sb_w_qkv, sb_bias, sb_w_o):
    b, s, d = x_prompt.shape
    n, t, _ = x_sample.shape
    depth = w_mod.shape[0]
    row = lambda a: a.reshape(a.shape[0], 1, a.shape[-1])
    p = dict(
        g_pre_mix=row(g_pre_mix), g_post_mix=row(g_post_mix), g_pre_ffn=row(g_pre_ffn), g_post_ffn=row(g_post_ffn),
        ffn_w_up=ffn_w_up.astype(BF16), ffn_w_conv=ffn_w_conv, ffn_b_conv=row(ffn_b_conv),
        ffn_w_down=ffn_w_down.astype(BF16),
        cf_w1=cf_w1.astype(BF16), cf_b1=row(cf_b1), cf_w_dw=cf_w_dw, cf_b_dw=row(cf_b_dw), cf_ln_g=row(cf_ln_g),
        cf_ln_b=row(cf_ln_b), cf_w2=cf_w2.astype(BF16), cf_b2=row(cf_b2),
        sc_w_in=sc_w_in.astype(BF16), sc_w_conv=sc_w_conv, sc_w_out=sc_w_out.astype(BF16),
        sb_w_qkv=sb_w_qkv.astype(BF16), sb_w_o=sb_w_o.astype(BF16))

    rows = b + n
    rows_pad = -(-rows // SUBLANES) * SUBLANES
    c_all = jnp.concatenate([c_prompt, c_sample, jnp.zeros((rows_pad - rows, d), F32)], axis=0)
    mod = _modulation(c_all, w_mod, b_mod).reshape(depth, rows_pad, 6, d)
    mod_p, mod_s = mod[:, :b], mod[:, b:b + n]

    y_p, cf_p, sc_p, ffn_p, k_p, v_p = _trunk(
        False, x_prompt, mod_p, p, None, None, None,
        lambda j, q, k, v: _sb_prompt(q, k, v, sb_bias[j]))

    pool = cache_k.shape[1]
    ck = cache_k.reshape(cache_k.shape[0], pool, PAGE_SIZE, d)
    cv = cache_v.reshape(cache_v.shape[0], pool, PAGE_SIZE, d)

    def attend_sample(j, q, k, v):
        o = _sb_sample(q.reshape(n, t, d), k.reshape(n, t, d), v.reshape(n, t, d), ck[j], cv[j],
                       page_table, sb_bias[j])
        return o.reshape(n * t, d)

    y_s, cf_s, sc_s, ffn_s, k_s, v_s = _trunk(
        True, x_sample.reshape(n * t, d), mod_s, p, state_cf_conv, state_sc_conv, state_ffn_conv, attend_sample)

    def last(bufs, width):
        return jnp.stack([u[:, u.shape[1] - (width - 1):] for u in bufs])

    def rolled(bufs, state, width):
        outs = []
        for u, st in zip(bufs, state):
            xp = jnp.concatenate([st, u.reshape(n, t, u.shape[-1])], axis=1)
            outs.append(xp[:, xp.shape[1] - (width - 1):])
        return jnp.stack(outs)

    heads = lambda a, lead: jnp.stack(a).reshape((len(a),) + lead + (N_HEADS, HEAD_DIM))
    return (y_p, y_s.reshape(n, t, d),
            heads(k_p, (b, s // PAGE_SIZE, PAGE_SIZE)), heads(v_p, (b, s // PAGE_SIZE, PAGE_SIZE)),
            heads(k_s, (n, t)), heads(v_s, (n, t)),
            last(cf_p, CF_WIDTH), rolled(cf_s, state_cf_conv, CF_WIDTH),
            last(sc_p, SC_WIDTH), rolled(sc_s, state_sc_conv, SC_WIDTH),
            last(ffn_p, FFN_WIDTH), rolled(ffn_s, state_ffn_conv, FFN_WIDTH))
```

```python
import functools
import math

import jax
import jax.numpy as jnp
from jax import lax
from jax.experimental import pallas as pl
from jax.experimental.pallas import tpu as pltpu

F32 = jnp.float32
BF16 = jnp.bfloat16

D_MODEL = 1024
N_HEADS = 16
HEAD_DIM = 64
PAGE_SIZE = 128
RMS_EPS = 1e-6
LN_EPS = 1e-5
LOG2E = 1.4426950408889634

SUBLANES = 8
LANES = 128
VMEM_LIMIT = 56 * 1024 * 1024

TM_PROMPT = 512
TM_SAMPLE = 256
TQ = 512
TK = 256
HEAD_PAIR = 2 * HEAD_DIM


def _cparams(sem):
    return pltpu.CompilerParams(dimension_semantics=sem, vmem_limit_bytes=VMEM_LIMIT)


def _const_spec(shape):
    nd = len(shape)
    return pl.BlockSpec(shape, lambda *_: (0,) * nd, pipeline_mode=pl.Buffered(1))


def _rms(x, g):
    return x * lax.rsqrt(jnp.mean(x * x, axis=-1, keepdims=True) + RMS_EPS) * g


def _ln(x, g, b):
    xc = x - jnp.mean(x, axis=-1, keepdims=True)
    return xc * lax.rsqrt(jnp.mean(xc * xc, axis=-1, keepdims=True) + LN_EPS) * g + b


def _sigmoid(x):
    return 1.0 / (1.0 + jnp.exp(-x))


def _mm(a, w):
    return jnp.dot(a.astype(BF16), w, preferred_element_type=F32)


def _conv_prompt(ext_ref, u, w_ref, width, first):
    tm = u.shape[0]
    halo = ext_ref.shape[0] - tm

    @pl.when(first)
    def _():
        ext_ref[0:halo, :] = jnp.zeros((halo, ext_ref.shape[1]), F32)

    ext_ref[halo:halo + tm, :] = u
    y = None
    for k in range(width):
        off = halo - (width - 1) + k
        term = w_ref[k:k + 1, :] * ext_ref[off:off + tm, :]
        y = term if y is None else y + term
    tail = ext_ref[tm:tm + halo, :]
    ext_ref[0:halo, :] = tail
    return y, tail


def _conv_sample(hist_ref, u, wj_ref, width):
    tb = hist_ref.shape[0]
    c = u.shape[1]
    u3 = u.reshape(tb, SUBLANES, c)
    y = None
    for j in range(width - 1 + SUBLANES):
        if j < width - 1:
            row = hist_ref[:, j:j + 1, :]
        else:
            row = u3[:, j - (width - 1):j - (width - 1) + 1, :]
        term = wj_ref[j][None, :, :] * row
        y = term if y is None else y + term
    return y.reshape(tb * SUBLANES, c)


def _premod(x, g_ref, sh, sc):
    return _rms(x, g_ref[...]) * (1.0 + sc) + sh


def _postmod(x, out, g_ref, gate):
    return x + gate * _rms(out, g_ref[...])


def _modvals(refs, sample):
    return [r[...] if sample else r[0] for r in refs]


def _mod_kernel(c_ref, w_ref, b_ref, o_ref):
    c = c_ref[...]
    act = c * _sigmoid(c)
    o_ref[0] = _mm(act, w_ref[0].astype(BF16)) + b_ref[0]


def _modulation(c_all, w_mod, b_mod):
    depth, d, n = w_mod.shape
    rows = c_all.shape[0]
    tn = 1536
    return pl.pallas_call(
        _mod_kernel,
        grid=(depth, n // tn),
        in_specs=[
            pl.BlockSpec((rows, d), lambda i, j: (0, 0)),
            pl.BlockSpec((1, d, tn), lambda i, j: (i, 0, j)),
            pl.BlockSpec((1, 1, tn), lambda i, j: (i, 0, j)),
        ],
        out_specs=pl.BlockSpec((1, rows, tn), lambda i, j: (i, 0, j)),
        out_shape=jax.ShapeDtypeStruct((depth, rows, n), F32),
        compiler_params=_cparams(("arbitrary", "arbitrary")),
        name="adaln_modulation",
    )(c_all, w_mod, b_mod.reshape(depth, 1, n))


def _row_call(kernel, sample, x, mods, consts, extra_in, outs, scratch, name):
    d = x.shape[-1]
    if sample:
        tm = TM_SAMPLE
        grid = (x.shape[0] // tm,)
        x_spec = pl.BlockSpec((tm, d), lambda i: (i, 0))
        mod_specs = [pl.BlockSpec((tm, d), lambda i: (i, 0)) for _ in mods]
        sem = ("arbitrary",)
    else:
        tm = TM_PROMPT
        grid = (x.shape[0], x.shape[1] // tm)
        x_spec = pl.BlockSpec((1, tm, d), lambda b, i: (b, i, 0))
        mod_specs = [pl.BlockSpec((1, 1, d), lambda b, i: (b, 0, 0)) for _ in mods]
        sem = ("arbitrary", "arbitrary")
    in_specs = [x_spec] + mod_specs + [_const_spec(c.shape) for c in consts] + [s for _, s in extra_in]
    args = [x] + list(mods) + list(consts) + [a for a, _ in extra_in]
    out_shape = [jax.ShapeDtypeStruct(s, dt) for s, dt, _ in outs]
    out_specs = [sp for _, _, sp in outs]
    return pl.pallas_call(
        kernel,
        grid=grid,
        in_specs=in_specs,
        out_specs=out_specs,
        out_shape=out_shape,
        scratch_shapes=scratch,
        compiler_params=_cparams(sem),
        name=name,
    )(*args)


def _x_out(x, sample):
    d = x.shape[-1]
    if sample:
        return (x.shape, F32, pl.BlockSpec((TM_SAMPLE, d), lambda i: (i, 0)))
    return (x.shape, F32, pl.BlockSpec((1, TM_PROMPT, d), lambda b, i: (b, i, 0)))


def _tail_out(batch, halo, c):
    return ((batch, halo, c), F32, pl.BlockSpec((1, halo, c), lambda b, i: (b, 0, 0)))


def _rows_out(rows, c):
    return ((rows, c), F32, pl.BlockSpec((TM_SAMPLE, c), lambda i: (i, 0)))


def _hist_in(state):
    _, w1, c = state.shape
    return (state, pl.BlockSpec((TM_SAMPLE // SUBLANES, w1, c), lambda i: (i, 0, 0)))


def _toeplitz(w):
    width, c = w.shape
    j = jnp.arange(width - 1 + SUBLANES)[:, None]
    t = jnp.arange(SUBLANES)[None, :]
    k = j - t
    valid = (k >= 0) & (k < width)
    return jnp.where(valid[:, :, None], w[jnp.clip(k, 0, width - 1)], 0.0)


CF_WIDTH = 31
CF_HALO = 32


def _cf_kernel(sample, x_ref, sh_ref, sc_ref, gt_ref, gpre_ref, gpost_ref, w1_ref, b1_ref, wdw_ref,
               bdw_ref, lng_ref, lnb_ref, w2_ref, b2_ref, *rest):
    if sample:
        hist_ref, xo_ref, u_ref = rest
    else:
        xo_ref, u_ref, ext_ref = rest
    sh, sc, gt = _modvals((sh_ref, sc_ref, gt_ref), sample)
    x = x_ref[...] if sample else x_ref[0]
    d = x.shape[1]
    h = _premod(x, gpre_ref, sh, sc)
    ag = _mm(h, w1_ref[...]) + b1_ref[...]
    u = ag[:, :d] * _sigmoid(ag[:, d:])
    if sample:
        y = _conv_sample(hist_ref, u, wdw_ref, CF_WIDTH)
        u_ref[...] = u
    else:
        y, tail = _conv_prompt(ext_ref, u, wdw_ref, CF_WIDTH, pl.program_id(1) == 0)
        u_ref[0] = tail
    y = _ln(y + bdw_ref[...], lng_ref[...], lnb_ref[...])
    y = y * _sigmoid(y)
    out = _mm(y, w2_ref[...]) + b2_ref[...]
    xn = _postmod(x, out, gpost_ref, gt)
    if sample:
        xo_ref[...] = xn
    else:
        xo_ref[0] = xn


def _cf_layer(sample, x, mods, gpre, gpost, w1, b1, wdw, bdw, lng, lnb, w2, b2, state=None):
    d = x.shape[-1]
    consts = [gpre, gpost, w1, b1, _toeplitz(wdw) if sample else wdw, bdw, lng, lnb, w2, b2]
    if sample:
        outs = [_x_out(x, True), _rows_out(x.shape[0], d)]
        extra, scratch = [_hist_in(state)], []
    else:
        outs = [_x_out(x, False), _tail_out(x.shape[0], CF_HALO, d)]
        extra, scratch = [], [pltpu.VMEM((CF_HALO + TM_PROMPT, d), F32)]
    return _row_call(functools.partial(_cf_kernel, sample), sample, x, mods, consts, extra, outs, scratch,
                     "conformer_mixer_sample" if sample else "conformer_mixer_prompt")


SC_WIDTH = 3
SHORT_HALO = 8


def _sc_kernel(sample, x_ref, sh_ref, sc_ref, gt_ref, gpre_ref, gpost_ref, win_ref, wcv_ref, wout_ref, *rest):
    if sample:
        hist_ref, xo_ref, u_ref = rest
    else:
        xo_ref, u_ref, ext_ref = rest
    sh, sc, gt = _modvals((sh_ref, sc_ref, gt_ref), sample)
    x = x_ref[...] if sample else x_ref[0]
    d = x.shape[1]
    h = _premod(x, gpre_ref, sh, sc)
    bcx = _mm(h, win_ref[...])
    cx = bcx[:, d:2 * d] * bcx[:, 2 * d:]
    if sample:
        y = _conv_sample(hist_ref, cx, wcv_ref, SC_WIDTH)
        u_ref[...] = cx
    else:
        y, tail = _conv_prompt(ext_ref, cx, wcv_ref, SC_WIDTH, pl.program_id(1) == 0)
        u_ref[0] = tail
    out = _mm(bcx[:, :d] * y, wout_ref[...])
    xn = _postmod(x, out, gpost_ref, gt)
    if sample:
        xo_ref[...] = xn
    else:
        xo_ref[0] = xn


def _sc_layer(sample, x, mods, gpre, gpost, win, wcv, wout, state=None):
    d = x.shape[-1]
    consts = [gpre, gpost, win, _toeplitz(wcv) if sample else wcv, wout]
    if sample:
        outs = [_x_out(x, True), _rows_out(x.shape[0], d)]
        extra, scratch = [_hist_in(state)], []
    else:
        outs = [_x_out(x, False), _tail_out(x.shape[0], SHORT_HALO, d)]
        extra, scratch = [], [pltpu.VMEM((SHORT_HALO + TM_PROMPT, d), F32)]
    return _row_call(functools.partial(_sc_kernel, sample), sample, x, mods, consts, extra, outs, scratch,
                     "shortconv_mixer_sample" if sample else "shortconv_mixer_prompt")


FFN_WIDTH = 3
FFN_CHUNK = 1408


def _ffn_kernel(sample, x_ref, sh_ref, sc_ref, gt_ref, gpre_ref, gpost_ref, wup_ref, wcv_ref, bcv_ref,
                wdn_ref, *rest):
    if sample:
        hist_ref, xo_ref, up_ref = rest
    else:
        xo_ref, up_ref, ext_ref, prev_ref = rest
    sh, sc, gt = _modvals((sh_ref, sc_ref, gt_ref), sample)
    x = x_ref[...] if sample else x_ref[0]
    tm = x.shape[0]
    f = wdn_ref.shape[0]
    h = _premod(x, gpre_ref, sh, sc).astype(BF16)
    out = None
    for c0 in range(0, f, FFN_CHUNK):
        halves = []
        for base in (c0, f + c0):
            cols = slice(base, base + FFN_CHUNK)
            up = jnp.dot(h, wup_ref[:, cols], preferred_element_type=F32)
            if sample:
                tb = tm // SUBLANES
                up3 = up.reshape(tb, SUBLANES, FFN_CHUNK)
                y = None
                for j in range(FFN_WIDTH - 1 + SUBLANES):
                    if j < FFN_WIDTH - 1:
                        row = hist_ref[:, j:j + 1, cols]
                    else:
                        row = up3[:, j - (FFN_WIDTH - 1):j - (FFN_WIDTH - 1) + 1, :]
                    term = wcv_ref[j, :, cols][None, :, :] * row
                    y = term if y is None else y + term
                y = y.reshape(tm, FFN_CHUNK)
                up_ref[:, cols] = up
            else:
                first = pl.program_id(1) == 0

                @pl.when(first)
                def _():
                    prev_ref[:, cols] = jnp.zeros((SHORT_HALO, FFN_CHUNK), F32)

                ext_ref[0:SHORT_HALO, :] = prev_ref[:, cols]
                ext_ref[SHORT_HALO:SHORT_HALO + tm, :] = up
                y = None
                for k in range(FFN_WIDTH):
                    off = SHORT_HALO - (FFN_WIDTH - 1) + k
                    term = wcv_ref[k:k + 1, cols] * ext_ref[off:off + tm, :]
                    y = term if y is None else y + term
                tail = up[tm - SHORT_HALO:, :]
                prev_ref[:, cols] = tail
                up_ref[0, :, cols] = tail
            halves.append(y + bcv_ref[:, cols])
        a, g = halves
        act = (g * _sigmoid(g)) * a
        part = _mm(act, wdn_ref[c0:c0 + FFN_CHUNK, :])
        out = part if out is None else out + part
    xn = _postmod(x, out, gpost_ref, gt)
    if sample:
        xo_ref[...] = xn
    else:
        xo_ref[0] = xn


def _ffn_layer(sample, x, mods, gpre, gpost, wup, wcv, bcv, wdn, state=None):
    c = wup.shape[1]
    consts = [gpre, gpost, wup, _toeplitz(wcv) if sample else wcv, bcv, wdn]
    if sample:
        outs = [_x_out(x, True), _rows_out(x.shape[0], c)]
        extra, scratch = [_hist_in(state)], []
    else:
        outs = [_x_out(x, False), _tail_out(x.shape[0], SHORT_HALO, c)]
        extra = []
        scratch = [pltpu.VMEM((SHORT_HALO + TM_PROMPT, FFN_CHUNK), F32), pltpu.VMEM((SHORT_HALO, c), F32)]
    return _row_call(functools.partial(_ffn_kernel, sample), sample, x, mods, consts, extra, outs, scratch,
                     "convffn_sample" if sample else "convffn_prompt")


def _qkv_kernel(sample, x_ref, sh_ref, sc_ref, gpre_ref, w_ref, q_ref, k_ref, v_ref):
    sh, sc = _modvals((sh_ref, sc_ref), sample)
    x = x_ref[...] if sample else x_ref[0]
    d = x.shape[1]
    h = _premod(x, gpre_ref, sh, sc)
    qkv = _mm(h, w_ref[...])
    for idx, ref in enumerate((q_ref, k_ref, v_ref)):
        val = qkv[:, idx * d:(idx + 1) * d]
        if sample:
            ref[...] = val
        else:
            ref[0] = val


def _qkv_layer(sample, x, mods, gpre, w):
    outs = [_x_out(x, sample)] * 3
    return _row_call(functools.partial(_qkv_kernel, sample), sample, x, mods, [gpre, w], [], outs, [],
                     "qkv_proj_sample" if sample else "qkv_proj_prompt")


def _attn_out_kernel(sample, x_ref, gt_ref, gpost_ref, w_ref, o_ref, xo_ref):
    (gt,) = _modvals((gt_ref,), sample)
    x = x_ref[...] if sample else x_ref[0]
    o = o_ref[...] if sample else o_ref[0]
    xn = _postmod(x, _mm(o, w_ref[...]), gpost_ref, gt)
    if sample:
        xo_ref[...] = xn
    else:
        xo_ref[0] = xn


def _attn_out_layer(sample, x, mods, gpost, w, o):
    _, _, spec = _x_out(x, sample)
    return _row_call(functools.partial(_attn_out_kernel, sample), sample, x, mods, [gpost, w], [(o, spec)],
                     [_x_out(x, sample)], [], "attn_out_sample" if sample else "attn_out_prompt")[0]


SP_CLAMP = 64.0
BIAS_ROWS = 3
BLOCKS_PER_STEP = 2


def _softplus2(z):
    return jnp.maximum(z, jnp.log2(1.0 + jnp.exp2(jnp.minimum(z, SP_CLAMP))))


def _split_bf16(x):
    hi = x.astype(BF16)
    lo = (x - hi.astype(F32)).astype(BF16)
    return hi, lo


def _sb_prompt_kernel(bias_ref, q_ref, k_ref, v_ref, tri_ref, o_ref, kb_ref, vt_ref, qt_ref,
                      acc_ref, carry_ref):
    p = pl.program_id(1)
    i = pl.program_id(2)
    nblk = kb_ref.shape[0]
    qsub = TQ // TK

    @pl.when(i == 0)
    def _():
        ones_cols = (lax.broadcasted_iota(jnp.int32, (TK, HEAD_PAIR), 1) < BIAS_ROWS).astype(BF16)

        def cast(j, carry):
            rows = pl.ds(pl.multiple_of(j * TK, TK), TK)
            kb_ref[j, :, 0:HEAD_PAIR] = k_ref[0, rows, :].astype(BF16)
            kb_ref[j, :, HEAD_PAIR:2 * HEAD_PAIR] = ones_cols
            vt_ref[j] = v_ref[0, rows, :].T.astype(BF16)
            return carry
        lax.fori_loop(0, nblk, cast, 0)

    lane = lax.broadcasted_iota(jnp.int32, (1, HEAD_PAIR), 1)
    q = q_ref[0] * (HEAD_DIM ** -0.5 * LOG2E)
    qt_ref[0:HEAD_PAIR, 0:TQ] = jnp.where(lane < HEAD_DIM, q, 0.0).T.astype(BF16)
    qt_ref[0:HEAD_PAIR, TQ:2 * TQ] = jnp.where(lane >= HEAD_DIM, q, 0.0).T.astype(BF16)
    col = lax.broadcasted_iota(jnp.int32, (SUBLANES, 2 * TQ), 1)
    rowi = lax.broadcasted_iota(jnp.int32, (SUBLANES, 2 * TQ), 0)
    rest = jnp.where(col < TQ, bias_ref[2 * p], bias_ref[2 * p + 1]) * LOG2E
    parts = jnp.zeros((SUBLANES, 2 * TQ), F32)
    for r in range(BIAS_ROWS):
        part = rest.astype(BF16).astype(F32)
        parts = jnp.where(rowi == r, part, parts)
        rest = rest - part
    qt_ref[HEAD_PAIR:HEAD_PAIR + SUBLANES, :] = parts.astype(BF16)
    qt_ref[HEAD_PAIR + SUBLANES:, :] = jnp.zeros((HEAD_PAIR - SUBLANES, 2 * TQ), BF16)
    acc_ref[...] = jnp.zeros_like(acc_ref)
    carry_ref[...] = jnp.zeros_like(carry_ref)

    def visit(items):
        n = len(items)
        half = TK // 2
        mask = (lax.broadcasted_iota(jnp.int32, (TK, TK), 0) < lax.broadcasted_iota(jnp.int32, (TK, TK), 1))
        cols = [slice(s * TK, (s + 1) * TK) for _, s, _ in items]
        last_of_strip = {s: k for k, (_, s, _) in enumerate(items)}
        zs, lbs, splits, halves = {}, {}, {}, {}
        seen, probs = {}, {}

        def scores(k):
            zs[k] = jnp.dot(kb_ref[items[k][0]], qt_ref[:, cols[k]], preferred_element_type=F32)

        def softplus(k):
            z = zs.pop(k)
            sp = _softplus2(z)
            lbs[k] = z - sp
            spm = jnp.where(mask, sp, 0.0) if items[k][2] else sp
            splits[k] = [jnp.concatenate(_split_bf16(spm[h0:h0 + half]), axis=0) for h0 in (half, 0)]

        def suffix(k):
            out = []
            for hilo in splits.pop(k):
                res = jnp.dot(tri_ref[...], hilo, preferred_element_type=F32)
                out.append((res[0:half], res[half:half + 1]))
            halves[k] = out

        def finish(k):
            j, s, diag = items[k]
            c = cols[k]
            lb = lbs.pop(k)
            (later_new, tot_new), (later_old, tot_old) = halves.pop(k)
            before = carry_ref[:, c] if s not in seen else carry_ref[:, c] + seen[s]
            a_new = jnp.exp2(lb[half:] - later_new - before)
            a_old = jnp.exp2(lb[:half] - later_old - (before + tot_new))
            a = jnp.concatenate([a_old, a_new], axis=0)
            if diag:
                a = jnp.where(mask, a, 0.0)
            probs.setdefault(s, []).append((j, a.astype(BF16)))
            tot = tot_new + tot_old
            seen[s] = tot if s not in seen else seen[s] + tot
            if last_of_strip[s] == k:
                parts = probs.pop(s)
                vblk = jnp.concatenate([vt_ref[jb] for jb, _ in parts], axis=1) if len(parts) > 1 else vt_ref[j]
                ablk = jnp.concatenate([ab for _, ab in parts], axis=0) if len(parts) > 1 else parts[0][1]
                acc_ref[:, c] += jnp.dot(vblk, ablk, preferred_element_type=F32)
                carry_ref[:, c] += seen[s]

        stages = (scores, softplus, suffix, finish)
        for t in range(n + len(stages) - 1):
            for d, stage in enumerate(stages):
                if 0 <= t - d < n:
                    stage(t - d)

    visit([(i * qsub + kb, hh * qsub + qs, qs == kb)
           for kb in reversed(range(qsub)) for hh in range(2) for qs in range(kb, qsub)])

    n_old = i * qsub
    n_rem = n_old % BLOCKS_PER_STEP
    for rem in range(qsub, BLOCKS_PER_STEP, qsub):
        @pl.when(n_rem == rem)
        def _(rem=rem):
            visit([(n_old - 1 - r, s, False) for r in range(rem) for s in range(2 * qsub)])

    def body(jj, carry):
        j = n_old - n_rem - 1 - jj * BLOCKS_PER_STEP
        visit([(j - r, s, False) for r in range(BLOCKS_PER_STEP) for s in range(2 * qsub)])
        return carry

    lax.fori_loop(0, n_old // BLOCKS_PER_STEP, body, 0)

    acc = acc_ref[...]
    row = lax.broadcasted_iota(jnp.int32, (HEAD_PAIR, 1), 0)
    out_t = jnp.where(row < HEAD_DIM, acc[:, 0:TQ], acc[:, TQ:2 * TQ])
    o_ref[0] = out_t.T


def _sb_prompt(q, k, v, bias):
    b, s, d = q.shape
    nblk = s // TK
    kk = jnp.arange(TK // 2 + SUBLANES)[:, None]
    tri = ((jnp.arange(TK // 2)[None, :] > kk) | (kk == TK // 2)).astype(BF16)
    tri2 = jnp.concatenate([tri, tri], axis=1)
    grid_spec = pltpu.PrefetchScalarGridSpec(
        num_scalar_prefetch=1,
        grid=(b, d // HEAD_PAIR, s // TQ),
        in_specs=[
            pl.BlockSpec((1, TQ, HEAD_PAIR), lambda bi, p, i, *_: (bi, i, p)),
            pl.BlockSpec((1, s, HEAD_PAIR), lambda bi, p, i, *_: (bi, 0, p)),
            pl.BlockSpec((1, s, HEAD_PAIR), lambda bi, p, i, *_: (bi, 0, p)),
            pl.BlockSpec((TK // 2 + SUBLANES, TK), lambda bi, p, i, *_: (0, 0)),
        ],
        out_specs=pl.BlockSpec((1, TQ, HEAD_PAIR), lambda bi, p, i, *_: (bi, i, p)),
        scratch_shapes=[
            pltpu.VMEM((nblk, TK, 2 * HEAD_PAIR), BF16),
            pltpu.VMEM((nblk, HEAD_PAIR, TK), BF16),
            pltpu.VMEM((2 * HEAD_PAIR, 2 * TQ), BF16),
            pltpu.VMEM((HEAD_PAIR, 2 * TQ), F32),
            pltpu.VMEM((1, 2 * TQ), F32),
        ],
    )
    return pl.pallas_call(
        _sb_prompt_kernel,
        grid_spec=grid_spec,
        out_shape=jax.ShapeDtypeStruct((b, s, d), F32),
        compiler_params=_cparams(("arbitrary", "arbitrary", "arbitrary")),
        name="stickbreaking_prompt",
    )(bias, q, k, v, tri2)


PAGES_PER_STEP = 4
CACHE_ROWS = PAGE_SIZE * N_HEADS


def _page_rows(ref):
    pairs = []
    for hp in range(N_HEADS // 2):
        even = ref[0, 0, pl.ds(2 * hp, PAGE_SIZE, stride=N_HEADS), :]
        odd = ref[0, 0, pl.ds(2 * hp + 1, PAGE_SIZE, stride=N_HEADS), :]
        pairs.append(jnp.concatenate([even, odd], axis=1))
    return jnp.concatenate(pairs, axis=1).astype(BF16)


def _sb_sample_kernel(pt_ref, q_ref, kn_ref, vn_ref, *rest):
    kc_refs = rest[:PAGES_PER_STEP]
    vc_refs = rest[PAGES_PER_STEP:2 * PAGES_PER_STEP]
    bias_ref, tri_ref, o_ref, qe_ref, acc_ref, carry_ref = rest[2 * PAGES_PER_STEP:]
    g = pl.program_id(1)
    nsteps = pl.num_programs(1)
    d = q_ref.shape[2]
    rows = N_HEADS * SUBLANES

    def visit(blocks):
        n = len(blocks)
        t = lax.broadcasted_iota(jnp.int32, (rows, PAGE_SIZE), 0) % SUBLANES
        mask = lax.broadcasted_iota(jnp.int32, (rows, PAGE_SIZE), 1) < t
        zs, lbs, hilos, sums, vals = {}, {}, {}, {}, {}
        state = {"before": carry_ref[...], "acc": None}

        def scores(k):
            kb = blocks[k][0]()
            vals[k] = blocks[k][1]()
            z = lax.dot_general(qe_ref[...], kb, (((1,), (1,)), ((), ())), preferred_element_type=F32)
            zs[k] = z + bias_ref[...]

        def softplus(k):
            z = zs.pop(k)
            sp = _softplus2(z)
            lbs[k] = z - sp
            spm = jnp.where(mask, sp, 0.0) if blocks[k][2] else sp
            hilos[k] = jnp.concatenate(_split_bf16(spm), axis=1)

        def suffix(k):
            sums[k] = jnp.dot(hilos.pop(k), tri_ref[...], preferred_element_type=F32)

        def finish(k):
            res = sums.pop(k)
            a = jnp.exp2(lbs.pop(k) - res[:, 0:PAGE_SIZE] - state["before"])
            if blocks[k][2]:
                a = jnp.where(mask, a, 0.0)
            state["before"] = state["before"] + res[:, PAGE_SIZE:]
            part = jnp.dot(a.astype(BF16), vals.pop(k), preferred_element_type=F32)
            state["acc"] = part if state["acc"] is None else state["acc"] + part

        stages = (scores, softplus, suffix, finish)
        for step in range(n + len(stages) - 1):
            for dd, stage in enumerate(stages):
                if 0 <= step - dd < n:
                    stage(step - dd)
        carry_ref[...] = state["before"]
        acc_ref[...] += state["acc"]

    @pl.when(g == 0)
    def _():
        q = q_ref[0] * (HEAD_DIM ** -0.5 * LOG2E)
        qrep = jnp.concatenate([q] * N_HEADS, axis=0)
        head_of_row = lax.broadcasted_iota(jnp.int32, (rows, d), 0) // SUBLANES
        head_of_col = lax.broadcasted_iota(jnp.int32, (rows, d), 1) // HEAD_DIM
        qe_ref[...] = jnp.where(head_of_row == head_of_col, qrep, 0.0).astype(BF16)
        acc_ref[...] = jnp.zeros_like(acc_ref)
        carry_ref[...] = jnp.zeros_like(carry_ref)
        pad = jnp.zeros((PAGE_SIZE - SUBLANES, d), F32)
        visit([(lambda: jnp.concatenate([kn_ref[0], pad], axis=0).astype(BF16),
                lambda: jnp.concatenate([vn_ref[0], pad], axis=0).astype(BF16), True)])

    visit([(functools.partial(_page_rows, kc_refs[r]), functools.partial(_page_rows, vc_refs[r]), False)
           for r in range(PAGES_PER_STEP)])

    @pl.when(g == nsteps - 1)
    def _():
        acc = acc_ref[...]
        head_of_col = lax.broadcasted_iota(jnp.int32, (SUBLANES, d), 1) // HEAD_DIM
        out = jnp.zeros((SUBLANES, d), F32)
        for hh in range(N_HEADS):
            out = out + jnp.where(head_of_col == hh, acc[hh * SUBLANES:(hh + 1) * SUBLANES, :], 0.0)
        o_ref[0] = out


def _sb_sample(q, k, v, cache_k, cache_v, layer, page_table, bias):
    n, t, d = q.shape
    n_pages = page_table.shape[1]
    rows = N_HEADS * SUBLANES
    bias_rows = jnp.broadcast_to(jnp.repeat(bias * LOG2E, SUBLANES)[:, None], (rows, PAGE_SIZE)).astype(F32)
    kk = jnp.arange(PAGE_SIZE)
    tri = jnp.concatenate([(kk[:, None] > kk[None, :]), jnp.ones((PAGE_SIZE, PAGE_SIZE), bool)], axis=1)
    tri2 = jnp.concatenate([tri, tri], axis=0).astype(BF16)

    def page_spec(r):
        return pl.BlockSpec((1, 1, CACHE_ROWS, HEAD_DIM),
                            lambda ni, gi, pt: (layer, pt[ni * n_pages + (n_pages - 1 - gi * PAGES_PER_STEP - r)], 0, 0))

    seq_spec = pl.BlockSpec((1, t, d), lambda ni, gi, pt: (ni, 0, 0))
    grid_spec = pltpu.PrefetchScalarGridSpec(
        num_scalar_prefetch=1,
        grid=(n, n_pages // PAGES_PER_STEP),
        in_specs=[seq_spec, seq_spec, seq_spec]
        + [page_spec(r) for r in range(PAGES_PER_STEP)] * 2
        + [pl.BlockSpec((rows, PAGE_SIZE), lambda ni, gi, pt: (0, 0)),
           pl.BlockSpec((2 * PAGE_SIZE, 2 * PAGE_SIZE), lambda ni, gi, pt: (0, 0))],
        out_specs=seq_spec,
        scratch_shapes=[
            pltpu.VMEM((rows, d), BF16),
            pltpu.VMEM((rows, d), F32),
            pltpu.VMEM((rows, PAGE_SIZE), F32),
        ],
    )
    return pl.pallas_call(
        _sb_sample_kernel,
        grid_spec=grid_spec,
        out_shape=jax.ShapeDtypeStruct((n, t, d), F32),
        compiler_params=_cparams(("arbitrary", "arbitrary")),
        name="stickbreaking_sample",
    )(page_table.reshape(-1), q, k, v, *([cache_k] * PAGES_PER_STEP), *([cache_v] * PAGES_PER_STEP), bias_rows, tri2)


def _trunk(sample, x, mod, p, cf_state, sc_state, ffn_state, attend):
    depth = mod.shape[0]
    cf_new, sc_new, ffn_new, k_new, v_new = [], [], [], [], []
    for i in range(depth):
        kind, j = i % 3, i // 3
        if sample:
            m = [jnp.repeat(mod[i, :, c, :], SUBLANES, axis=0) for c in range(6)]
        else:
            m = [mod[i, :, c:c + 1, :] for c in range(6)]
        sh1, sc1, g1, sh2, sc2, g2 = m
        if kind == 0:
            x, nbuf = _cf_layer(sample, x, (sh1, sc1, g1), p['g_pre_mix'][i], p['g_post_mix'][i],
                                p['cf_w1'][j], p['cf_b1'][j], p['cf_w_dw'][j], p['cf_b_dw'][j],
                                p['cf_ln_g'][j], p['cf_ln_b'][j], p['cf_w2'][j], p['cf_b2'][j],
                                None if not sample else cf_state[j])
            cf_new.append(nbuf)
        elif kind == 1:
            x, nbuf = _sc_layer(sample, x, (sh1, sc1, g1), p['g_pre_mix'][i], p['g_post_mix'][i],
                                p['sc_w_in'][j], p['sc_w_conv'][j], p['sc_w_out'][j],
                                None if not sample else sc_state[j])
            sc_new.append(nbuf)
        else:
            q, k, v = _qkv_layer(sample, x, (sh1, sc1), p['g_pre_mix'][i], p['sb_w_qkv'][j])
            o = attend(j, q, k, v)
            x = _attn_out_layer(sample, x, (g1,), p['g_post_mix'][i], p['sb_w_o'][j], o)
            k_new.append(k)
            v_new.append(v)
        x, nbuf = _ffn_layer(sample, x, (sh2, sc2, g2), p['g_pre_ffn'][i], p['g_post_ffn'][i],
                             p['ffn_w_up'][i], p['ffn_w_conv'][i], p['ffn_b_conv'][i], p['ffn_w_down'][i],
                             None if not sample else ffn_state[i])
        ffn_new.append(nbuf)
    return x, cf_new, sc_new, ffn_new, k_new, v_new


def kernel(x_prompt, x_sample, c_prompt, c_sample, state_cf_conv, state_sc_conv, state_ffn_conv, cache_k, cache_v, page_table, g_pre_mix, g_post_mix, g_pre_ffn, g_post_ffn, w_mod, b_mod, ffn_w_up, ffn_w_conv, ffn_b_conv, ffn_w_down, cf_w1, cf_b1, cf_w_dw, cf_b_dw, cf_ln_g, cf_ln_b, cf_w2, cf_b2, sc_w_in, sc_w_conv, sc_w_out, sb_w_qkv, sb_bias, sb_w_o):
    b, s, d = x_prompt.shape
    n, t, _ = x_sample.shape
    depth = w_mod.shape[0]
    row = lambda a: a.reshape(a.shape[0], 1, a.shape[-1])
    p = dict(
        g_pre_mix=row(g_pre_mix), g_post_mix=row(g_post_mix), g_pre_ffn=row(g_pre_ffn), g_post_ffn=row(g_post_ffn),
        ffn_w_up=ffn_w_up.astype(BF16), ffn_w_conv=ffn_w_conv, ffn_b_conv=row(ffn_b_conv),
        ffn_w_down=ffn_w_down.astype(BF16),
        cf_w1=cf_w1.astype(BF16), cf_b1=row(cf_b1), cf_w_dw=cf_w_dw, cf_b_dw=row(cf_b_dw), cf_ln_g=row(cf_ln_g),
        cf_ln_b=row(cf_ln_b), cf_w2=cf_w2.astype(BF16), cf_b2=row(cf_b2),
        sc_w_in=sc_w_in.astype(BF16), sc_w_conv=sc_w_conv, sc_w_out=sc_w_out.astype(BF16),
        sb_w_qkv=sb_w_qkv.astype(BF16), sb_w_o=sb_w_o.astype(BF16))

    rows = b + n
    rows_pad = -(-rows // SUBLANES) * SUBLANES
    c_all = jnp.concatenate([c_prompt, c_sample, jnp.zeros((rows_pad - rows, d), F32)], axis=0)
    mod = _modulation(c_all, w_mod, b_mod).reshape(depth, rows_pad, 6, d)
    mod_p, mod_s = mod[:, :b], mod[:, b:b + n]

    y_p, cf_p, sc_p, ffn_p, k_p, v_p = _trunk(
        False, x_prompt, mod_p, p, None, None, None,
        lambda j, q, k, v: _sb_prompt(q, k, v, sb_bias[j]))

    pool = cache_k.shape[1]
    ck = cache_k.reshape(cache_k.shape[0], pool, CACHE_ROWS, HEAD_DIM)
    cv = cache_v.reshape(cache_v.shape[0], pool, CACHE_ROWS, HEAD_DIM)

    def attend_sample(j, q, k, v):
        o = _sb_sample(q.reshape(n, t, d), k.reshape(n, t, d), v.reshape(n, t, d), ck, cv, j,
                       page_table, sb_bias[j])
        return o.reshape(n * t, d)

    y_s, cf_s, sc_s, ffn_s, k_s, v_s = _trunk(
        True, x_sample.reshape(n * t, d), mod_s, p, state_cf_conv, state_sc_conv, state_ffn_conv, attend_sample)

    def last(bufs, width):
        return jnp.stack([u[:, u.shape[1] - (width - 1):] for u in bufs])

    def rolled(bufs, state, width):
        outs = []
        for u, st in zip(bufs, state):
            xp = jnp.concatenate([st, u.reshape(n, t, u.shape[-1])], axis=1)
            outs.append(xp[:, xp.shape[1] - (width - 1):])
        return jnp.stack(outs)

    heads = lambda a, lead: jnp.stack(a).reshape((len(a),) + lead + (N_HEADS, HEAD_DIM))
    return (y_p, y_s.reshape(n, t, d),
            heads(k_p, (b, s // PAGE_SIZE, PAGE_SIZE)), heads(v_p, (b, s // PAGE_SIZE, PAGE_SIZE)),
            heads(k_s, (n, t)), heads(v_s, (n, t)),
            last(cf_p, CF_WIDTH), rolled(cf_s, state_cf_conv, CF_WIDTH),
            last(sc_p, SC_WIDTH), rolled(sc_s, state_sc_conv, SC_WIDTH),
            last(ffn_p, FFN_WIDTH), rolled(ffn_s, state_ffn_conv, FFN_WIDTH))
```

```python
import functools
import math

import jax
import jax.numpy as jnp
from jax import lax
from jax.experimental import pallas as pl
from jax.experimental.pallas import tpu as pltpu

F32 = jnp.float32
BF16 = jnp.bfloat16

D_MODEL = 1024
N_HEADS = 16
HEAD_DIM = 64
PAGE_SIZE = 128
RMS_EPS = 1e-6
LN_EPS = 1e-5
LOG2E = 1.4426950408889634

SUBLANES = 8
LANES = 128
VMEM_LIMIT = 56 * 1024 * 1024

TM_PROMPT = 512
TM_SAMPLE = 256
TQ = 512
TK = 256
HEAD_PAIR = 2 * HEAD_DIM


def _cparams(sem):
    return pltpu.CompilerParams(dimension_semantics=sem, vmem_limit_bytes=VMEM_LIMIT)


def _const_spec(shape):
    nd = len(shape)
    return pl.BlockSpec(shape, lambda *_: (0,) * nd, pipeline_mode=pl.Buffered(1))


def _rms(x, g):
    return x * lax.rsqrt(jnp.mean(x * x, axis=-1, keepdims=True) + RMS_EPS) * g


def _ln(x, g, b):
    xc = x - jnp.mean(x, axis=-1, keepdims=True)
    return xc * lax.rsqrt(jnp.mean(xc * xc, axis=-1, keepdims=True) + LN_EPS) * g + b


def _sigmoid(x):
    return 1.0 / (1.0 + jnp.exp(-x))


def _mm(a, w):
    return jnp.dot(a.astype(BF16), w, preferred_element_type=F32)


def _conv_prompt(ext_ref, u, w_ref, width, first):
    tm = u.shape[0]
    halo = ext_ref.shape[0] - tm

    @pl.when(first)
    def _():
        ext_ref[0:halo, :] = jnp.zeros((halo, ext_ref.shape[1]), F32)

    ext_ref[halo:halo + tm, :] = u
    y = None
    for k in range(width):
        off = halo - (width - 1) + k
        term = w_ref[k:k + 1, :] * ext_ref[off:off + tm, :]
        y = term if y is None else y + term
    tail = ext_ref[tm:tm + halo, :]
    ext_ref[0:halo, :] = tail
    return y, tail


def _conv_sample(hist_ref, u, wj_ref, width):
    tb = hist_ref.shape[0]
    c = u.shape[1]
    u3 = u.reshape(tb, SUBLANES, c)
    y = None
    for j in range(width - 1 + SUBLANES):
        if j < width - 1:
            row = hist_ref[:, j:j + 1, :]
        else:
            row = u3[:, j - (width - 1):j - (width - 1) + 1, :]
        term = wj_ref[j][None, :, :] * row
        y = term if y is None else y + term
    return y.reshape(tb * SUBLANES, c)


def _premod(x, g_ref, sh, sc):
    return _rms(x, g_ref[...]) * (1.0 + sc) + sh


def _postmod(x, out, g_ref, gate):
    return x + gate * _rms(out, g_ref[...])


def _modvals(refs, sample):
    return [r[...] if sample else r[0] for r in refs]


def _mod_kernel(c_ref, w_ref, b_ref, o_ref):
    c = c_ref[...]
    act = c * _sigmoid(c)
    o_ref[0] = _mm(act, w_ref[0].astype(BF16)) + b_ref[0]


def _modulation(c_all, w_mod, b_mod):
    depth, d, n = w_mod.shape
    rows = c_all.shape[0]
    tn = 1536
    return pl.pallas_call(
        _mod_kernel,
        grid=(depth, n // tn),
        in_specs=[
            pl.BlockSpec((rows, d), lambda i, j: (0, 0)),
            pl.BlockSpec((1, d, tn), lambda i, j: (i, 0, j)),
            pl.BlockSpec((1, 1, tn), lambda i, j: (i, 0, j)),
        ],
        out_specs=pl.BlockSpec((1, rows, tn), lambda i, j: (i, 0, j)),
        out_shape=jax.ShapeDtypeStruct((depth, rows, n), F32),
        compiler_params=_cparams(("arbitrary", "arbitrary")),
        name="adaln_modulation",
    )(c_all, w_mod, b_mod.reshape(depth, 1, n))


def _row_call(kernel, sample, x, mods, consts, extra_in, outs, scratch, name):
    d = x.shape[-1]
    if sample:
        tm = TM_SAMPLE
        grid = (x.shape[0] // tm,)
        x_spec = pl.BlockSpec((tm, d), lambda i: (i, 0))
        mod_specs = [pl.BlockSpec((tm, d), lambda i: (i, 0)) for _ in mods]
        sem = ("arbitrary",)
    else:
        tm = TM_PROMPT
        grid = (x.shape[0], x.shape[1] // tm)
        x_spec = pl.BlockSpec((1, tm, d), lambda b, i: (b, i, 0))
        mod_specs = [pl.BlockSpec((1, 1, d), lambda b, i: (b, 0, 0)) for _ in mods]
        sem = ("arbitrary", "arbitrary")
    in_specs = [x_spec] + mod_specs + [_const_spec(c.shape) for c in consts] + [s for _, s in extra_in]
    args = [x] + list(mods) + list(consts) + [a for a, _ in extra_in]
    out_shape = [jax.ShapeDtypeStruct(s, dt) for s, dt, _ in outs]
    out_specs = [sp for _, _, sp in outs]
    return pl.pallas_call(
        kernel,
        grid=grid,
        in_specs=in_specs,
        out_specs=out_specs,
        out_shape=out_shape,
        scratch_shapes=scratch,
        compiler_params=_cparams(sem),
        name=name,
    )(*args)


def _x_out(x, sample):
    d = x.shape[-1]
    if sample:
        return (x.shape, F32, pl.BlockSpec((TM_SAMPLE, d), lambda i: (i, 0)))
    return (x.shape, F32, pl.BlockSpec((1, TM_PROMPT, d), lambda b, i: (b, i, 0)))


def _tail_out(batch, halo, c):
    return ((batch, halo, c), F32, pl.BlockSpec((1, halo, c), lambda b, i: (b, 0, 0)))


def _rows_out(rows, c):
    return ((rows, c), F32, pl.BlockSpec((TM_SAMPLE, c), lambda i: (i, 0)))


def _hist_in(state):
    _, w1, c = state.shape
    return (state, pl.BlockSpec((TM_SAMPLE // SUBLANES, w1, c), lambda i: (i, 0, 0)))


def _toeplitz(w):
    width, c = w.shape
    j = jnp.arange(width - 1 + SUBLANES)[:, None]
    t = jnp.arange(SUBLANES)[None, :]
    k = j - t
    valid = (k >= 0) & (k < width)
    return jnp.where(valid[:, :, None], w[jnp.clip(k, 0, width - 1)], 0.0)


CF_WIDTH = 31
CF_HALO = 32


def _cf_kernel(sample, x_ref, sh_ref, sc_ref, gt_ref, gpre_ref, gpost_ref, w1_ref, b1_ref, wdw_ref,
               bdw_ref, lng_ref, lnb_ref, w2_ref, b2_ref, *rest):
    if sample:
        hist_ref, xo_ref, u_ref = rest
    else:
        xo_ref, u_ref, ext_ref = rest
    sh, sc, gt = _modvals((sh_ref, sc_ref, gt_ref), sample)
    x = x_ref[...] if sample else x_ref[0]
    d = x.shape[1]
    h = _premod(x, gpre_ref, sh, sc)
    ag = _mm(h, w1_ref[...]) + b1_ref[...]
    u = ag[:, :d] * _sigmoid(ag[:, d:])
    if sample:
        y = _conv_sample(hist_ref, u, wdw_ref, CF_WIDTH)
        u_ref[...] = u
    else:
        y, tail = _conv_prompt(ext_ref, u, wdw_ref, CF_WIDTH, pl.program_id(1) == 0)
        u_ref[0] = tail
    y = _ln(y + bdw_ref[...], lng_ref[...], lnb_ref[...])
    y = y * _sigmoid(y)
    out = _mm(y, w2_ref[...]) + b2_ref[...]
    xn = _postmod(x, out, gpost_ref, gt)
    if sample:
        xo_ref[...] = xn
    else:
        xo_ref[0] = xn


def _cf_layer(sample, x, mods, gpre, gpost, w1, b1, wdw, bdw, lng, lnb, w2, b2, state=None):
    d = x.shape[-1]
    consts = [gpre, gpost, w1, b1, _toeplitz(wdw) if sample else wdw, bdw, lng, lnb, w2, b2]
    if sample:
        outs = [_x_out(x, True), _rows_out(x.shape[0], d)]
        extra, scratch = [_hist_in(state)], []
    else:
        outs = [_x_out(x, False), _tail_out(x.shape[0], CF_HALO, d)]
        extra, scratch = [], [pltpu.VMEM((CF_HALO + TM_PROMPT, d), F32)]
    return _row_call(functools.partial(_cf_kernel, sample), sample, x, mods, consts, extra, outs, scratch,
                     "conformer_mixer_sample" if sample else "conformer_mixer_prompt")


SC_WIDTH = 3
SHORT_HALO = 8


def _sc_kernel(sample, x_ref, sh_ref, sc_ref, gt_ref, gpre_ref, gpost_ref, win_ref, wcv_ref, wout_ref, *rest):
    if sample:
        hist_ref, xo_ref, u_ref = rest
    else:
        xo_ref, u_ref, ext_ref = rest
    sh, sc, gt = _modvals((sh_ref, sc_ref, gt_ref), sample)
    x = x_ref[...] if sample else x_ref[0]
    d = x.shape[1]
    h = _premod(x, gpre_ref, sh, sc)
    bcx = _mm(h, win_ref[...])
    cx = bcx[:, d:2 * d] * bcx[:, 2 * d:]
    if sample:
        y = _conv_sample(hist_ref, cx, wcv_ref, SC_WIDTH)
        u_ref[...] = cx
    else:
        y, tail = _conv_prompt(ext_ref, cx, wcv_ref, SC_WIDTH, pl.program_id(1) == 0)
        u_ref[0] = tail
    out = _mm(bcx[:, :d] * y, wout_ref[...])
    xn = _postmod(x, out, gpost_ref, gt)
    if sample:
        xo_ref[...] = xn
    else:
        xo_ref[0] = xn


def _sc_layer(sample, x, mods, gpre, gpost, win, wcv, wout, state=None):
    d = x.shape[-1]
    consts = [gpre, gpost, win, _toeplitz(wcv) if sample else wcv, wout]
    if sample:
        outs = [_x_out(x, True), _rows_out(x.shape[0], d)]
        extra, scratch = [_hist_in(state)], []
    else:
        outs = [_x_out(x, False), _tail_out(x.shape[0], SHORT_HALO, d)]
        extra, scratch = [], [pltpu.VMEM((SHORT_HALO + TM_PROMPT, d), F32)]
    return _row_call(functools.partial(_sc_kernel, sample), sample, x, mods, consts, extra, outs, scratch,
                     "shortconv_mixer_sample" if sample else "shortconv_mixer_prompt")


FFN_WIDTH = 3
FFN_CHUNK = 1408


def _ffn_kernel(sample, x_ref, sh_ref, sc_ref, gt_ref, gpre_ref, gpost_ref, wup_ref, wcv_ref, bcv_ref,
                wdn_ref, *rest):
    if sample:
        hist_ref, xo_ref, up_ref = rest
    else:
        xo_ref, up_ref, ext_ref, prev_ref = rest
    sh, sc, gt = _modvals((sh_ref, sc_ref, gt_ref), sample)
    x = x_ref[...] if sample else x_ref[0]
    tm = x.shape[0]
    f = wdn_ref.shape[0]
    h = _premod(x, gpre_ref, sh, sc).astype(BF16)
    out = None
    for c0 in range(0, f, FFN_CHUNK):
        halves = []
        for base in (c0, f + c0):
            cols = slice(base, base + FFN_CHUNK)
            up = jnp.dot(h, wup_ref[:, cols], preferred_element_type=F32)
            if sample:
                tb = tm // SUBLANES
                up3 = up.reshape(tb, SUBLANES, FFN_CHUNK)
                y = None
                for j in range(FFN_WIDTH - 1 + SUBLANES):
                    if j < FFN_WIDTH - 1:
                        row = hist_ref[:, j:j + 1, cols]
                    else:
                        row = up3[:, j - (FFN_WIDTH - 1):j - (FFN_WIDTH - 1) + 1, :]
                    term = wcv_ref[j, :, cols][None, :, :] * row
                    y = term if y is None else y + term
                y = y.reshape(tm, FFN_CHUNK)
                up_ref[:, cols] = up
            else:
                first = pl.program_id(1) == 0

                @pl.when(first)
                def _():
                    prev_ref[:, cols] = jnp.zeros((SHORT_HALO, FFN_CHUNK), F32)

                ext_ref[0:SHORT_HALO, :] = prev_ref[:, cols]
                ext_ref[SHORT_HALO:SHORT_HALO + tm, :] = up
                y = None
                for k in range(FFN_WIDTH):
                    off = SHORT_HALO - (FFN_WIDTH - 1) + k
                    term = wcv_ref[k:k + 1, cols] * ext_ref[off:off + tm, :]
                    y = term if y is None else y + term
                tail = up[tm - SHORT_HALO:, :]
                prev_ref[:, cols] = tail
                up_ref[0, :, cols] = tail
            halves.append(y + bcv_ref[:, cols])
        a, g = halves
        act = (g * _sigmoid(g)) * a
        part = _mm(act, wdn_ref[c0:c0 + FFN_CHUNK, :])
        out = part if out is None else out + part
    xn = _postmod(x, out, gpost_ref, gt)
    if sample:
        xo_ref[...] = xn
    else:
        xo_ref[0] = xn


def _ffn_layer(sample, x, mods, gpre, gpost, wup, wcv, bcv, wdn, state=None):
    c = wup.shape[1]
    consts = [gpre, gpost, wup, _toeplitz(wcv) if sample else wcv, bcv, wdn]
    if sample:
        outs = [_x_out(x, True), _rows_out(x.shape[0], c)]
        extra, scratch = [_hist_in(state)], []
    else:
        outs = [_x_out(x, False), _tail_out(x.shape[0], SHORT_HALO, c)]
        extra = []
        scratch = [pltpu.VMEM((SHORT_HALO + TM_PROMPT, FFN_CHUNK), F32), pltpu.VMEM((SHORT_HALO, c), F32)]
    return _row_call(functools.partial(_ffn_kernel, sample), sample, x, mods, consts, extra, outs, scratch,
                     "convffn_sample" if sample else "convffn_prompt")


def _qkv_kernel(sample, x_ref, sh_ref, sc_ref, gpre_ref, w_ref, q_ref, k_ref, v_ref):
    sh, sc = _modvals((sh_ref, sc_ref), sample)
    x = x_ref[...] if sample else x_ref[0]
    d = x.shape[1]
    h = _premod(x, gpre_ref, sh, sc)
    qkv = _mm(h, w_ref[...])
    if sample:
        for idx, ref in enumerate((q_ref, k_ref, v_ref)):
            ref[...] = qkv[:, idx * d:(idx + 1) * d]
    else:
        q_ref[0] = qkv[:, 0:d]
        for idx, ref in ((1, k_ref), (2, v_ref)):
            for pg in range(x.shape[0] // PAGE_SIZE):
                ref[0, pg] = qkv[pg * PAGE_SIZE:(pg + 1) * PAGE_SIZE, idx * d:(idx + 1) * d].T


def _qkv_layer(sample, x, mods, gpre, w):
    if sample:
        outs = [_x_out(x, True)] * 3
    else:
        b, s, d = x.shape
        pages = TM_PROMPT // PAGE_SIZE
        paged = ((b, s // PAGE_SIZE, d, PAGE_SIZE), F32,
                 pl.BlockSpec((1, pages, d, PAGE_SIZE), lambda bi, i: (bi, i, 0, 0)))
        outs = [_x_out(x, False), paged, paged]
    return _row_call(functools.partial(_qkv_kernel, sample), sample, x, mods, [gpre, w], [], outs, [],
                     "qkv_proj_sample" if sample else "qkv_proj_prompt")


def _attn_out_kernel(sample, x_ref, gt_ref, gpost_ref, w_ref, o_ref, xo_ref):
    (gt,) = _modvals((gt_ref,), sample)
    x = x_ref[...] if sample else x_ref[0]
    o = o_ref[...] if sample else o_ref[0]
    xn = _postmod(x, _mm(o, w_ref[...]), gpost_ref, gt)
    if sample:
        xo_ref[...] = xn
    else:
        xo_ref[0] = xn


def _attn_out_layer(sample, x, mods, gpost, w, o):
    _, _, spec = _x_out(x, sample)
    return _row_call(functools.partial(_attn_out_kernel, sample), sample, x, mods, [gpost, w], [(o, spec)],
                     [_x_out(x, sample)], [], "attn_out_sample" if sample else "attn_out_prompt")[0]


SP_CLAMP = 64.0
BIAS_ROWS = 3
BLOCKS_PER_STEP = 2


def _softplus2(z):
    return jnp.maximum(z, jnp.log2(1.0 + jnp.exp2(jnp.minimum(z, SP_CLAMP))))


def _split_bf16(x):
    hi = x.astype(BF16)
    lo = (x - hi.astype(F32)).astype(BF16)
    return hi, lo


def _sb_prompt_kernel(bias_ref, q_ref, k_ref, v_ref, tri_ref, o_ref, kb_ref, vt_ref, qt_ref,
                      acc_ref, carry_ref):
    p = pl.program_id(1)
    i = pl.program_id(2)
    nblk = kb_ref.shape[0]
    qsub = TQ // TK

    @pl.when(i == 0)
    def _():
        ones_cols = (lax.broadcasted_iota(jnp.int32, (TK, HEAD_PAIR), 1) < BIAS_ROWS).astype(BF16)

        ppb = TK // PAGE_SIZE

        def cast(j, carry):
            kt = jnp.concatenate([k_ref[0, j * ppb + r] for r in range(ppb)], axis=1)
            kb_ref[j, :, 0:HEAD_PAIR] = kt.T.astype(BF16)
            kb_ref[j, :, HEAD_PAIR:2 * HEAD_PAIR] = ones_cols
            vt_ref[j] = jnp.concatenate([v_ref[0, j * ppb + r] for r in range(ppb)], axis=1).astype(BF16)
            return carry
        lax.fori_loop(0, nblk, cast, 0)

    lane = lax.broadcasted_iota(jnp.int32, (1, HEAD_PAIR), 1)
    q = q_ref[0] * (HEAD_DIM ** -0.5 * LOG2E)
    qt_ref[0:HEAD_PAIR, 0:TQ] = jnp.where(lane < HEAD_DIM, q, 0.0).T.astype(BF16)
    qt_ref[0:HEAD_PAIR, TQ:2 * TQ] = jnp.where(lane >= HEAD_DIM, q, 0.0).T.astype(BF16)
    col = lax.broadcasted_iota(jnp.int32, (SUBLANES, 2 * TQ), 1)
    rowi = lax.broadcasted_iota(jnp.int32, (SUBLANES, 2 * TQ), 0)
    rest = jnp.where(col < TQ, bias_ref[2 * p], bias_ref[2 * p + 1]) * LOG2E
    parts = jnp.zeros((SUBLANES, 2 * TQ), F32)
    for r in range(BIAS_ROWS):
        part = rest.astype(BF16).astype(F32)
        parts = jnp.where(rowi == r, part, parts)
        rest = rest - part
    qt_ref[HEAD_PAIR:HEAD_PAIR + SUBLANES, :] = parts.astype(BF16)
    qt_ref[HEAD_PAIR + SUBLANES:, :] = jnp.zeros((HEAD_PAIR - SUBLANES, 2 * TQ), BF16)
    acc_ref[...] = jnp.zeros_like(acc_ref)
    carry_ref[...] = jnp.zeros_like(carry_ref)

    def visit(items):
        n = len(items)
        half = TK // 2
        mask = (lax.broadcasted_iota(jnp.int32, (TK, TK), 0) < lax.broadcasted_iota(jnp.int32, (TK, TK), 1))
        cols = [slice(s * TK, (s + 1) * TK) for _, s, _ in items]
        last_of_strip = {s: k for k, (_, s, _) in enumerate(items)}
        zs, lbs, splits, halves = {}, {}, {}, {}
        seen, probs = {}, {}

        def scores(k):
            zs[k] = jnp.dot(kb_ref[items[k][0]], qt_ref[:, cols[k]], preferred_element_type=F32)

        def softplus(k):
            z = zs.pop(k)
            sp = _softplus2(z)
            lbs[k] = z - sp
            spm = jnp.where(mask, sp, 0.0) if items[k][2] else sp
            splits[k] = [jnp.concatenate(_split_bf16(spm[h0:h0 + half]), axis=0) for h0 in (half, 0)]

        def suffix(k):
            out = []
            for hilo in splits.pop(k):
                res = jnp.dot(tri_ref[...], hilo, preferred_element_type=F32)
                out.append((res[0:half], res[half:half + 1]))
            halves[k] = out

        def finish(k):
            j, s, diag = items[k]
            c = cols[k]
            lb = lbs.pop(k)
            (later_new, tot_new), (later_old, tot_old) = halves.pop(k)
            before = carry_ref[:, c] if s not in seen else carry_ref[:, c] + seen[s]
            a_new = jnp.exp2(lb[half:] - later_new - before)
            a_old = jnp.exp2(lb[:half] - later_old - (before + tot_new))
            a = jnp.concatenate([a_old, a_new], axis=0)
            if diag:
                a = jnp.where(mask, a, 0.0)
            probs.setdefault(s, []).append((j, a.astype(BF16)))
            tot = tot_new + tot_old
            seen[s] = tot if s not in seen else seen[s] + tot
            if last_of_strip[s] == k:
                parts = probs.pop(s)
                vblk = jnp.concatenate([vt_ref[jb] for jb, _ in parts], axis=1) if len(parts) > 1 else vt_ref[j]
                ablk = jnp.concatenate([ab for _, ab in parts], axis=0) if len(parts) > 1 else parts[0][1]
                acc_ref[:, c] += jnp.dot(vblk, ablk, preferred_element_type=F32)
                carry_ref[:, c] += seen[s]

        stages = (scores, softplus, suffix, finish)
        for t in range(n + len(stages) - 1):
            for d, stage in enumerate(stages):
                if 0 <= t - d < n:
                    stage(t - d)

    visit([(i * qsub + kb, hh * qsub + qs, qs == kb)
           for kb in reversed(range(qsub)) for hh in range(2) for qs in range(kb, qsub)])

    n_old = i * qsub
    n_rem = n_old % BLOCKS_PER_STEP
    for rem in range(qsub, BLOCKS_PER_STEP, qsub):
        @pl.when(n_rem == rem)
        def _(rem=rem):
            visit([(n_old - 1 - r, s, False) for r in range(rem) for s in range(2 * qsub)])

    def body(jj, carry):
        j = n_old - n_rem - 1 - jj * BLOCKS_PER_STEP
        visit([(j - r, s, False) for r in range(BLOCKS_PER_STEP) for s in range(2 * qsub)])
        return carry

    lax.fori_loop(0, n_old // BLOCKS_PER_STEP, body, 0)

    acc = acc_ref[...]
    row = lax.broadcasted_iota(jnp.int32, (HEAD_PAIR, 1), 0)
    out_t = jnp.where(row < HEAD_DIM, acc[:, 0:TQ], acc[:, TQ:2 * TQ])
    o_ref[0] = out_t.T


def _sb_prompt(q, k, v, bias):
    b, s, d = q.shape
    nblk = s // TK
    paged = pl.BlockSpec((1, s // PAGE_SIZE, HEAD_PAIR, PAGE_SIZE), lambda bi, p, i, *_: (bi, 0, p, 0))
    kk = jnp.arange(TK // 2 + SUBLANES)[:, None]
    tri = ((jnp.arange(TK // 2)[None, :] > kk) | (kk == TK // 2)).astype(BF16)
    tri2 = jnp.concatenate([tri, tri], axis=1)
    grid_spec = pltpu.PrefetchScalarGridSpec(
        num_scalar_prefetch=1,
        grid=(b, d // HEAD_PAIR, s // TQ),
        in_specs=[
            pl.BlockSpec((1, TQ, HEAD_PAIR), lambda bi, p, i, *_: (bi, i, p)),
            paged,
            paged,
            pl.BlockSpec((TK // 2 + SUBLANES, TK), lambda bi, p, i, *_: (0, 0)),
        ],
        out_specs=pl.BlockSpec((1, TQ, HEAD_PAIR), lambda bi, p, i, *_: (bi, i, p)),
        scratch_shapes=[
            pltpu.VMEM((nblk, TK, 2 * HEAD_PAIR), BF16),
            pltpu.VMEM((nblk, HEAD_PAIR, TK), BF16),
            pltpu.VMEM((2 * HEAD_PAIR, 2 * TQ), BF16),
            pltpu.VMEM((HEAD_PAIR, 2 * TQ), F32),
            pltpu.VMEM((1, 2 * TQ), F32),
        ],
    )
    return pl.pallas_call(
        _sb_prompt_kernel,
        grid_spec=grid_spec,
        out_shape=jax.ShapeDtypeStruct((b, s, d), F32),
        compiler_params=_cparams(("arbitrary", "arbitrary", "arbitrary")),
        name="stickbreaking_prompt",
    )(bias, q, k, v, tri2)


def _sb_sample_kernel(n_pages, pt_ref, q_ref, kn_ref, vn_ref, *rest):
    kc_refs = rest[:n_pages]
    vc_refs = rest[n_pages:2 * n_pages]
    bias_ref, tri_ref, o_ref = rest[2 * n_pages:]
    d = q_ref.shape[2]
    rows = N_HEADS * SUBLANES
    contract_lanes = (((1,), (1,)), ((), ()))

    q = q_ref[0] * (HEAD_DIM ** -0.5 * LOG2E)
    qrep = jnp.concatenate([q] * N_HEADS, axis=0)
    head_of_row = lax.broadcasted_iota(jnp.int32, (rows, d), 0) // SUBLANES
    head_of_col = lax.broadcasted_iota(jnp.int32, (rows, d), 1) // HEAD_DIM
    qe = jnp.where(head_of_row == head_of_col, qrep, 0.0).astype(BF16)
    t = lax.broadcasted_iota(jnp.int32, (rows, PAGE_SIZE), 0) % SUBLANES
    mask = lax.broadcasted_iota(jnp.int32, (rows, PAGE_SIZE), 1) < t
    bias = bias_ref[...]
    tri = tri_ref[...]

    n = n_pages + 1
    zs, lbs, hilos, sums, probs = {}, {}, {}, {}, {}
    state = {"before": jnp.zeros((rows, PAGE_SIZE), F32), "acc": None}

    def values_t(k):
        if k == 0:
            pad = jnp.zeros((PAGE_SIZE - SUBLANES, d), F32)
            return jnp.concatenate([vn_ref[0], pad], axis=0).astype(BF16).T
        return vc_refs[n_pages - k][0, 0].astype(BF16)

    def scores(k):
        if k == 0:
            pad = jnp.zeros((PAGE_SIZE - SUBLANES, d), F32)
            kb = jnp.concatenate([kn_ref[0], pad], axis=0).astype(BF16)
            z = lax.dot_general(qe, kb, contract_lanes, preferred_element_type=F32)
        else:
            z = jnp.dot(qe, kc_refs[n_pages - k][0, 0].astype(BF16), preferred_element_type=F32)
        zs[k] = z + bias

    def softplus(k):
        z = zs.pop(k)
        sp = _softplus2(z)
        lbs[k] = z - sp
        spm = jnp.where(mask, sp, 0.0) if k == 0 else sp
        hilos[k] = jnp.concatenate(_split_bf16(spm), axis=1)

    def suffix(k):
        sums[k] = jnp.dot(hilos.pop(k), tri, preferred_element_type=F32)

    def finish(k):
        res = sums.pop(k)
        a = jnp.exp2(lbs.pop(k) - res[:, 0:PAGE_SIZE] - state["before"])
        if k == 0:
            a = jnp.where(mask, a, 0.0)
        state["before"] = state["before"] + res[:, PAGE_SIZE:]
        probs[k] = a.astype(BF16)
        if k % 2 == 1 or k == n - 1:
            ks = sorted(probs)
            ablk = jnp.concatenate([probs.pop(kk) for kk in ks], axis=1) if len(ks) > 1 else probs.pop(ks[0])
            vblk = jnp.concatenate([values_t(kk) for kk in ks], axis=1) if len(ks) > 1 else values_t(ks[0])
            part = lax.dot_general(ablk, vblk, contract_lanes, preferred_element_type=F32)
            state["acc"] = part if state["acc"] is None else state["acc"] + part

    stages = (scores, softplus, suffix, finish)
    for step in range(n + len(stages) - 1):
        for dd, stage in enumerate(stages):
            if 0 <= step - dd < n:
                stage(step - dd)

    acc = state["acc"]
    col_head = lax.broadcasted_iota(jnp.int32, (SUBLANES, d), 1) // HEAD_DIM
    out = jnp.zeros((SUBLANES, d), F32)
    for hh in range(N_HEADS):
        out = out + jnp.where(col_head == hh, acc[hh * SUBLANES:(hh + 1) * SUBLANES, :], 0.0)
    o_ref[0] = out


def _sb_sample(q, k, v, cache_k, cache_v, layer, page_table, bias):
    n, t, d = q.shape
    n_pages = page_table.shape[1]
    rows = N_HEADS * SUBLANES
    bias_rows = jnp.broadcast_to(jnp.repeat(bias * LOG2E, SUBLANES)[:, None], (rows, PAGE_SIZE)).astype(F32)
    kk = jnp.arange(PAGE_SIZE)
    tri = jnp.concatenate([(kk[:, None] > kk[None, :]), jnp.ones((PAGE_SIZE, PAGE_SIZE), bool)], axis=1)
    tri2 = jnp.concatenate([tri, tri], axis=0).astype(BF16)

    def page_spec(r):
        return pl.BlockSpec((1, 1, d, PAGE_SIZE), lambda ni, pt: (layer, pt[ni * n_pages + r], 0, 0))

    seq_spec = pl.BlockSpec((1, t, d), lambda ni, pt: (ni, 0, 0))
    grid_spec = pltpu.PrefetchScalarGridSpec(
        num_scalar_prefetch=1,
        grid=(n,),
        in_specs=[seq_spec, seq_spec, seq_spec]
        + [page_spec(r) for r in range(n_pages)] * 2
        + [pl.BlockSpec((rows, PAGE_SIZE), lambda ni, pt: (0, 0)),
           pl.BlockSpec((2 * PAGE_SIZE, 2 * PAGE_SIZE), lambda ni, pt: (0, 0))],
        out_specs=seq_spec,
    )
    return pl.pallas_call(
        functools.partial(_sb_sample_kernel, n_pages),
        grid_spec=grid_spec,
        out_shape=jax.ShapeDtypeStruct((n, t, d), F32),
        compiler_params=_cparams(("arbitrary",)),
        name="stickbreaking_sample",
    )(page_table.reshape(-1), q, k, v, *([cache_k] * n_pages), *([cache_v] * n_pages), bias_rows, tri2)


def _trunk(sample, x, mod, p, cf_state, sc_state, ffn_state, attend):
    depth = mod.shape[0]
    cf_new, sc_new, ffn_new, k_new, v_new = [], [], [], [], []
    for i in range(depth):
        kind, j = i % 3, i // 3
        if sample:
            m = [jnp.repeat(mod[i, :, c, :], SUBLANES, axis=0) for c in range(6)]
        else:
            m = [mod[i, :, c:c + 1, :] for c in range(6)]
        sh1, sc1, g1, sh2, sc2, g2 = m
        if kind == 0:
            x, nbuf = _cf_layer(sample, x, (sh1, sc1, g1), p['g_pre_mix'][i], p['g_post_mix'][i],
                                p['cf_w1'][j], p['cf_b1'][j], p['cf_w_dw'][j], p['cf_b_dw'][j],
                                p['cf_ln_g'][j], p['cf_ln_b'][j], p['cf_w2'][j], p['cf_b2'][j],
                                None if not sample else cf_state[j])
            cf_new.append(nbuf)
        elif kind == 1:
            x, nbuf = _sc_layer(sample, x, (sh1, sc1, g1), p['g_pre_mix'][i], p['g_post_mix'][i],
                                p['sc_w_in'][j], p['sc_w_conv'][j], p['sc_w_out'][j],
                                None if not sample else sc_state[j])
            sc_new.append(nbuf)
        else:
            q, k, v = _qkv_layer(sample, x, (sh1, sc1), p['g_pre_mix'][i], p['sb_w_qkv'][j])
            o = attend(j, q, k, v)
            x = _attn_out_layer(sample, x, (g1,), p['g_post_mix'][i], p['sb_w_o'][j], o)
            k_new.append(k)
            v_new.append(v)
        x, nbuf = _ffn_layer(sample, x, (sh2, sc2, g2), p['g_pre_ffn'][i], p['g_post_ffn'][i],
                             p['ffn_w_up'][i], p['ffn_w_conv'][i], p['ffn_b_conv'][i], p['ffn_w_down'][i],
                             None if not sample else ffn_state[i])
        ffn_new.append(nbuf)
    return x, cf_new, sc_new, ffn_new, k_new, v_new


def kernel(x_prompt, x_sample, c_prompt, c_sample, state_cf_conv, state_sc_conv, state_ffn_conv, cache_k, cache_v, page_table, g_pre_mix, g_post_mix, g_pre_ffn, g_post_ffn, w_mod, b_mod, ffn_w_up, ffn_w_conv, ffn_b_conv, ffn_w_down, cf_w1, cf_b1, cf_w_dw, cf_b_dw, cf_ln_g, cf_ln_b, cf_w2, cf_b2, sc_w_in, sc_w_conv, sc_w_out, sb_w_qkv, sb_bias, sb_w_o):
    b, s, d = x_prompt.shape
    n, t, _ = x_sample.shape
    depth = w_mod.shape[0]
    row = lambda a: a.reshape(a.shape[0], 1, a.shape[-1])
    p = dict(
        g_pre_mix=row(g_pre_mix), g_post_mix=row(g_post_mix), g_pre_ffn=row(g_pre_ffn), g_post_ffn=row(g_post_ffn),
        ffn_w_up=ffn_w_up.astype(BF16), ffn_w_conv=ffn_w_conv, ffn_b_conv=row(ffn_b_conv),
        ffn_w_down=ffn_w_down.astype(BF16),
        cf_w1=cf_w1.astype(BF16), cf_b1=row(cf_b1), cf_w_dw=cf_w_dw, cf_b_dw=row(cf_b_dw), cf_ln_g=row(cf_ln_g),
        cf_ln_b=row(cf_ln_b), cf_w2=cf_w2.astype(BF16), cf_b2=row(cf_b2),
        sc_w_in=sc_w_in.astype(BF16), sc_w_conv=sc_w_conv, sc_w_out=sc_w_out.astype(BF16),
        sb_w_qkv=sb_w_qkv.astype(BF16), sb_w_o=sb_w_o.astype(BF16))

    rows = b + n
    rows_pad = -(-rows // SUBLANES) * SUBLANES
    c_all = jnp.concatenate([c_prompt, c_sample, jnp.zeros((rows_pad - rows, d), F32)], axis=0)
    mod = _modulation(c_all, w_mod, b_mod).reshape(depth, rows_pad, 6, d)
    mod_p, mod_s = mod[:, :b], mod[:, b:b + n]

    y_p, cf_p, sc_p, ffn_p, k_p, v_p = _trunk(
        False, x_prompt, mod_p, p, None, None, None,
        lambda j, q, k, v: _sb_prompt(q, k, v, sb_bias[j]))

    pool = cache_k.shape[1]
    ck = jnp.transpose(cache_k, (0, 1, 3, 4, 2)).reshape(cache_k.shape[0], pool, d, PAGE_SIZE)
    cv = jnp.transpose(cache_v, (0, 1, 3, 4, 2)).reshape(cache_v.shape[0], pool, d, PAGE_SIZE)

    def attend_sample(j, q, k, v):
        o = _sb_sample(q.reshape(n, t, d), k.reshape(n, t, d), v.reshape(n, t, d), ck, cv, j,
                       page_table, sb_bias[j])
        return o.reshape(n * t, d)

    y_s, cf_s, sc_s, ffn_s, k_s, v_s = _trunk(
        True, x_sample.reshape(n * t, d), mod_s, p, state_cf_conv, state_sc_conv, state_ffn_conv, attend_sample)

    def last(bufs, width):
        return jnp.stack([u[:, u.shape[1] - (width - 1):] for u in bufs])

    def rolled(bufs, state, width):
        outs = []
        for u, st in zip(bufs, state):
            xp = jnp.concatenate([st, u.reshape(n, t, u.shape[-1])], axis=1)
            outs.append(xp[:, xp.shape[1] - (width - 1):])
        return jnp.stack(outs)

    heads = lambda a, lead: jnp.stack(a).reshape((len(a),) + lead + (N_HEADS, HEAD_DIM))

    def pages(a):
        st = jnp.stack(a).reshape(len(a), b, s // PAGE_SIZE, N_HEADS, HEAD_DIM, PAGE_SIZE)
        return jnp.transpose(st, (0, 1, 2, 5, 3, 4))

    return (y_p, y_s.reshape(n, t, d),
            pages(k_p), pages(v_p),
            heads(k_s, (n, t)), heads(v_s, (n, t)),
            last(cf_p, CF_WIDTH), rolled(cf_s, state_cf_conv, CF_WIDTH),
            last(sc_p, SC_WIDTH), rolled(sc_s, state_sc_conv, SC_WIDTH),
            last(ffn_p, FFN_WIDTH), rolled(ffn_s, state_ffn_conv, FFN_WIDTH))
```

```python
import functools
import math

import jax
import jax.numpy as jnp
from jax import lax
from jax.experimental import pallas as pl
from jax.experimental.pallas import tpu as pltpu

F32 = jnp.float32
BF16 = jnp.bfloat16

D_MODEL = 1024
N_HEADS = 16
HEAD_DIM = 64
PAGE_SIZE = 128
RMS_EPS = 1e-6
LN_EPS = 1e-5
LOG2E = 1.4426950408889634

SUBLANES = 8
LANES = 128
VMEM_LIMIT = 56 * 1024 * 1024

TM_PROMPT = 512
TM_SAMPLE = 256
TQ = 512
TK = 256
HEAD_PAIR = 2 * HEAD_DIM


def _cparams(sem):
    return pltpu.CompilerParams(dimension_semantics=sem, vmem_limit_bytes=VMEM_LIMIT)


def _const_spec(shape):
    nd = len(shape)
    return pl.BlockSpec(shape, lambda *_: (0,) * nd, pipeline_mode=pl.Buffered(1))


def _rms(x, g):
    return x * lax.rsqrt(jnp.mean(x * x, axis=-1, keepdims=True) + RMS_EPS) * g


def _ln(x, g, b):
    xc = x - jnp.mean(x, axis=-1, keepdims=True)
    return xc * lax.rsqrt(jnp.mean(xc * xc, axis=-1, keepdims=True) + LN_EPS) * g + b


def _sigmoid(x):
    return 1.0 / (1.0 + jnp.exp(-x))


def _mm(a, w):
    return jnp.dot(a.astype(BF16), w, preferred_element_type=F32)


def _conv_prompt(ext_ref, u, w_ref, width, first):
    tm = u.shape[0]
    halo = ext_ref.shape[0] - tm

    @pl.when(first)
    def _():
        ext_ref[0:halo, :] = jnp.zeros((halo, ext_ref.shape[1]), F32)

    ext_ref[halo:halo + tm, :] = u
    y = None
    for k in range(width):
        off = halo - (width - 1) + k
        term = w_ref[k:k + 1, :] * ext_ref[off:off + tm, :]
        y = term if y is None else y + term
    tail = ext_ref[tm:tm + halo, :]
    ext_ref[0:halo, :] = tail
    return y, tail


def _conv_prompt_wide(ext_ref, u, w_ref, width, first):
    tm = u.shape[0]
    halo = ext_ref.shape[1] - tm

    @pl.when(first)
    def _():
        ext_ref[0, 0:halo, :] = jnp.zeros((halo, ext_ref.shape[2]), F32)

    ext_ref[0, halo:halo + tm, :] = u
    for r in range(1, SUBLANES):
        ext_ref[r, SUBLANES:halo + tm, :] = ext_ref[0, SUBLANES - r:halo + tm - r, :]
    y = None
    for k in range(width):
        back = width - 1 - k
        r = back % SUBLANES
        off = halo - (back - r)
        term = w_ref[k:k + 1, :] * ext_ref[r, off:off + tm, :]
        y = term if y is None else y + term
    tail = ext_ref[0, tm:tm + halo, :]
    ext_ref[0, 0:halo, :] = tail
    return y, tail


def _conv_sample(hist_ref, u, wj_ref, width):
    tb = hist_ref.shape[0]
    c = u.shape[1]
    u3 = u.reshape(tb, SUBLANES, c)
    y = None
    for j in range(width - 1 + SUBLANES):
        if j < width - 1:
            row = hist_ref[:, j:j + 1, :]
        else:
            row = u3[:, j - (width - 1):j - (width - 1) + 1, :]
        term = wj_ref[j][None, :, :] * row
        y = term if y is None else y + term
    return y.reshape(tb * SUBLANES, c)


def _premod(x, g_ref, sh, sc):
    return _rms(x, g_ref[...]) * (1.0 + sc) + sh


def _postmod(x, out, g_ref, gate):
    return x + gate * _rms(out, g_ref[...])


def _modvals(refs, sample):
    return [r[...] if sample else r[0] for r in refs]


def _mod_kernel(c_ref, w_ref, b_ref, o_ref):
    c = c_ref[...]
    act = c * _sigmoid(c)
    o_ref[0] = _mm(act, w_ref[0].astype(BF16)) + b_ref[0]


def _modulation(c_all, w_mod, b_mod):
    depth, d, n = w_mod.shape
    rows = c_all.shape[0]
    tn = 1536
    return pl.pallas_call(
        _mod_kernel,
        grid=(depth, n // tn),
        in_specs=[
            pl.BlockSpec((rows, d), lambda i, j: (0, 0)),
            pl.BlockSpec((1, d, tn), lambda i, j: (i, 0, j)),
            pl.BlockSpec((1, 1, tn), lambda i, j: (i, 0, j)),
        ],
        out_specs=pl.BlockSpec((1, rows, tn), lambda i, j: (i, 0, j)),
        out_shape=jax.ShapeDtypeStruct((depth, rows, n), F32),
        compiler_params=_cparams(("arbitrary", "arbitrary")),
        name="adaln_modulation",
    )(c_all, w_mod, b_mod.reshape(depth, 1, n))


def _row_call(kernel, sample, x, mods, consts, extra_in, outs, scratch, name):
    d = x.shape[-1]
    if sample:
        tm = TM_SAMPLE
        grid = (x.shape[0] // tm,)
        x_spec = pl.BlockSpec((tm, d), lambda i: (i, 0))
        mod_specs = [pl.BlockSpec((tm, d), lambda i: (i, 0)) for _ in mods]
        sem = ("arbitrary",)
    else:
        tm = TM_PROMPT
        grid = (x.shape[0], x.shape[1] // tm)
        x_spec = pl.BlockSpec((1, tm, d), lambda b, i: (b, i, 0))
        mod_specs = [pl.BlockSpec((1, 1, d), lambda b, i: (b, 0, 0)) for _ in mods]
        sem = ("arbitrary", "arbitrary")
    in_specs = [x_spec] + mod_specs + [_const_spec(c.shape) for c in consts] + [s for _, s in extra_in]
    args = [x] + list(mods) + list(consts) + [a for a, _ in extra_in]
    out_shape = [jax.ShapeDtypeStruct(s, dt) for s, dt, _ in outs]
    out_specs = [sp for _, _, sp in outs]
    return pl.pallas_call(
        kernel,
        grid=grid,
        in_specs=in_specs,
        out_specs=out_specs,
        out_shape=out_shape,
        scratch_shapes=scratch,
        compiler_params=_cparams(sem),
        name=name,
    )(*args)


def _x_out(x, sample):
    d = x.shape[-1]
    if sample:
        return (x.shape, F32, pl.BlockSpec((TM_SAMPLE, d), lambda i: (i, 0)))
    return (x.shape, F32, pl.BlockSpec((1, TM_PROMPT, d), lambda b, i: (b, i, 0)))


def _tail_out(batch, halo, c):
    return ((batch, halo, c), F32, pl.BlockSpec((1, halo, c), lambda b, i: (b, 0, 0)))


def _rows_out(rows, c):
    return ((rows, c), F32, pl.BlockSpec((TM_SAMPLE, c), lambda i: (i, 0)))


def _hist_in(state):
    _, w1, c = state.shape
    return (state, pl.BlockSpec((TM_SAMPLE // SUBLANES, w1, c), lambda i: (i, 0, 0)))


def _toeplitz(w):
    width, c = w.shape
    j = jnp.arange(width - 1 + SUBLANES)[:, None]
    t = jnp.arange(SUBLANES)[None, :]
    k = j - t
    valid = (k >= 0) & (k < width)
    return jnp.where(valid[:, :, None], w[jnp.clip(k, 0, width - 1)], 0.0)


CF_WIDTH = 31
CF_HALO = 32


def _cf_kernel(sample, x_ref, sh_ref, sc_ref, gt_ref, gpre_ref, gpost_ref, w1_ref, b1_ref, wdw_ref,
               bdw_ref, lng_ref, lnb_ref, w2_ref, b2_ref, *rest):
    if sample:
        hist_ref, xo_ref, u_ref = rest
    else:
        xo_ref, u_ref, ext_ref = rest
    sh, sc, gt = _modvals((sh_ref, sc_ref, gt_ref), sample)
    x = x_ref[...] if sample else x_ref[0]
    d = x.shape[1]
    h = _premod(x, gpre_ref, sh, sc)
    ag = _mm(h, w1_ref[...]) + b1_ref[...]
    u = ag[:, :d] * _sigmoid(ag[:, d:])
    if sample:
        y = _conv_sample(hist_ref, u, wdw_ref, CF_WIDTH)
        u_ref[...] = u
    else:
        y, tail = _conv_prompt_wide(ext_ref, u, wdw_ref, CF_WIDTH, pl.program_id(1) == 0)
        u_ref[0] = tail
    y = _ln(y + bdw_ref[...], lng_ref[...], lnb_ref[...])
    y = y * _sigmoid(y)
    out = _mm(y, w2_ref[...]) + b2_ref[...]
    xn = _postmod(x, out, gpost_ref, gt)
    if sample:
        xo_ref[...] = xn
    else:
        xo_ref[0] = xn


def _cf_layer(sample, x, mods, gpre, gpost, w1, b1, wdw, bdw, lng, lnb, w2, b2, state=None):
    d = x.shape[-1]
    consts = [gpre, gpost, w1, b1, _toeplitz(wdw) if sample else wdw, bdw, lng, lnb, w2, b2]
    if sample:
        outs = [_x_out(x, True), _rows_out(x.shape[0], d)]
        extra, scratch = [_hist_in(state)], []
    else:
        outs = [_x_out(x, False), _tail_out(x.shape[0], CF_HALO, d)]
        extra, scratch = [], [pltpu.VMEM((SUBLANES, CF_HALO + TM_PROMPT, d), F32)]
    return _row_call(functools.partial(_cf_kernel, sample), sample, x, mods, consts, extra, outs, scratch,
                     "conformer_mixer_sample" if sample else "conformer_mixer_prompt")


SC_WIDTH = 3
SHORT_HALO = 8


def _sc_kernel(sample, x_ref, sh_ref, sc_ref, gt_ref, gpre_ref, gpost_ref, win_ref, wcv_ref, wout_ref, *rest):
    if sample:
        hist_ref, xo_ref, u_ref = rest
    else:
        xo_ref, u_ref, ext_ref = rest
    sh, sc, gt = _modvals((sh_ref, sc_ref, gt_ref), sample)
    x = x_ref[...] if sample else x_ref[0]
    d = x.shape[1]
    h = _premod(x, gpre_ref, sh, sc)
    bcx = _mm(h, win_ref[...])
    cx = bcx[:, d:2 * d] * bcx[:, 2 * d:]
    if sample:
        y = _conv_sample(hist_ref, cx, wcv_ref, SC_WIDTH)
        u_ref[...] = cx
    else:
        y, tail = _conv_prompt(ext_ref, cx, wcv_ref, SC_WIDTH, pl.program_id(1) == 0)
        u_ref[0] = tail
    out = _mm(bcx[:, :d] * y, wout_ref[...])
    xn = _postmod(x, out, gpost_ref, gt)
    if sample:
        xo_ref[...] = xn
    else:
        xo_ref[0] = xn


def _sc_layer(sample, x, mods, gpre, gpost, win, wcv, wout, state=None):
    d = x.shape[-1]
    consts = [gpre, gpost, win, _toeplitz(wcv) if sample else wcv, wout]
    if sample:
        outs = [_x_out(x, True), _rows_out(x.shape[0], d)]
        extra, scratch = [_hist_in(state)], []
    else:
        outs = [_x_out(x, False), _tail_out(x.shape[0], SHORT_HALO, d)]
        extra, scratch = [], [pltpu.VMEM((SHORT_HALO + TM_PROMPT, d), F32)]
    return _row_call(functools.partial(_sc_kernel, sample), sample, x, mods, consts, extra, outs, scratch,
                     "shortconv_mixer_sample" if sample else "shortconv_mixer_prompt")


FFN_WIDTH = 3
FFN_CHUNK = 2816


def _ffn_kernel(sample, x_ref, sh_ref, sc_ref, gt_ref, gpre_ref, gpost_ref, wup_ref, wcv_ref, bcv_ref,
                wdn_ref, *rest):
    if sample:
        hist_ref, xo_ref, up_ref = rest
    else:
        xo_ref, up_ref, ext_ref, prev_ref = rest
    sh, sc, gt = _modvals((sh_ref, sc_ref, gt_ref), sample)
    x = x_ref[...] if sample else x_ref[0]
    tm = x.shape[0]
    f = wdn_ref.shape[0]
    h = _premod(x, gpre_ref, sh, sc).astype(BF16)
    out = None
    for c0 in range(0, f, FFN_CHUNK):
        halves = []
        for base in (c0, f + c0):
            cols = slice(base, base + FFN_CHUNK)
            up = jnp.dot(h, wup_ref[:, cols], preferred_element_type=F32)
            if sample:
                tb = tm // SUBLANES
                up3 = up.reshape(tb, SUBLANES, FFN_CHUNK)
                y = None
                for j in range(FFN_WIDTH - 1 + SUBLANES):
                    if j < FFN_WIDTH - 1:
                        row = hist_ref[:, j:j + 1, cols]
                    else:
                        row = up3[:, j - (FFN_WIDTH - 1):j - (FFN_WIDTH - 1) + 1, :]
                    term = wcv_ref[j, :, cols][None, :, :] * row
                    y = term if y is None else y + term
                y = y.reshape(tm, FFN_CHUNK)
                up_ref[:, cols] = up
            else:
                first = pl.program_id(1) == 0

                @pl.when(first)
                def _():
                    prev_ref[:, cols] = jnp.zeros((SHORT_HALO, FFN_CHUNK), F32)

                ext_ref[0:SHORT_HALO, :] = prev_ref[:, cols]
                ext_ref[SHORT_HALO:SHORT_HALO + tm, :] = up
                y = None
                for k in range(FFN_WIDTH):
                    off = SHORT_HALO - (FFN_WIDTH - 1) + k
                    term = wcv_ref[k:k + 1, cols] * ext_ref[off:off + tm, :]
                    y = term if y is None else y + term
                tail = up[tm - SHORT_HALO:, :]
                prev_ref[:, cols] = tail
                up_ref[0, :, cols] = tail
            halves.append(y + bcv_ref[:, cols])
        a, g = halves
        act = (g * _sigmoid(g)) * a
        part = _mm(act, wdn_ref[c0:c0 + FFN_CHUNK, :])
        out = part if out is None else out + part
    xn = _postmod(x, out, gpost_ref, gt)
    if sample:
        xo_ref[...] = xn
    else:
        xo_ref[0] = xn


def _ffn_layer(sample, x, mods, gpre, gpost, wup, wcv, bcv, wdn, state=None):
    c = wup.shape[1]
    consts = [gpre, gpost, wup, _toeplitz(wcv) if sample else wcv, bcv, wdn]
    if sample:
        outs = [_x_out(x, True), _rows_out(x.shape[0], c)]
        extra, scratch = [_hist_in(state)], []
    else:
        outs = [_x_out(x, False), _tail_out(x.shape[0], SHORT_HALO, c)]
        extra = []
        scratch = [pltpu.VMEM((SHORT_HALO + TM_PROMPT, FFN_CHUNK), F32), pltpu.VMEM((SHORT_HALO, c), F32)]
    return _row_call(functools.partial(_ffn_kernel, sample), sample, x, mods, consts, extra, outs, scratch,
                     "convffn_sample" if sample else "convffn_prompt")


def _qkv_kernel(sample, x_ref, sh_ref, sc_ref, gpre_ref, w_ref, q_ref, k_ref, v_ref):
    sh, sc = _modvals((sh_ref, sc_ref), sample)
    x = x_ref[...] if sample else x_ref[0]
    d = x.shape[1]
    h = _premod(x, gpre_ref, sh, sc)
    qkv = _mm(h, w_ref[...])
    if sample:
        for idx, ref in enumerate((q_ref, k_ref, v_ref)):
            ref[...] = qkv[:, idx * d:(idx + 1) * d]
    else:
        q_ref[0] = qkv[:, 0:d]
        for idx, ref in ((1, k_ref), (2, v_ref)):
            for pg in range(x.shape[0] // PAGE_SIZE):
                ref[0, pg] = qkv[pg * PAGE_SIZE:(pg + 1) * PAGE_SIZE, idx * d:(idx + 1) * d].T


def _qkv_layer(sample, x, mods, gpre, w):
    if sample:
        outs = [_x_out(x, True)] * 3
    else:
        b, s, d = x.shape
        pages = TM_PROMPT // PAGE_SIZE
        paged = ((b, s // PAGE_SIZE, d, PAGE_SIZE), F32,
                 pl.BlockSpec((1, pages, d, PAGE_SIZE), lambda bi, i: (bi, i, 0, 0)))
        outs = [_x_out(x, False), paged, paged]
    return _row_call(functools.partial(_qkv_kernel, sample), sample, x, mods, [gpre, w], [], outs, [],
                     "qkv_proj_sample" if sample else "qkv_proj_prompt")


def _attn_out_kernel(sample, x_ref, gt_ref, gpost_ref, w_ref, o_ref, xo_ref):
    (gt,) = _modvals((gt_ref,), sample)
    x = x_ref[...] if sample else x_ref[0]
    o = o_ref[...] if sample else o_ref[0]
    xn = _postmod(x, _mm(o, w_ref[...]), gpost_ref, gt)
    if sample:
        xo_ref[...] = xn
    else:
        xo_ref[0] = xn


def _attn_out_layer(sample, x, mods, gpost, w, o):
    _, _, spec = _x_out(x, sample)
    return _row_call(functools.partial(_attn_out_kernel, sample), sample, x, mods, [gpost, w], [(o, spec)],
                     [_x_out(x, sample)], [], "attn_out_sample" if sample else "attn_out_prompt")[0]


SP_CLAMP = 64.0
BIAS_ROWS = 3


def _softplus2(z):
    return jnp.maximum(z, jnp.log2(1.0 + jnp.exp2(jnp.minimum(z, SP_CLAMP))))


def _split_bf16(x):
    hi = x.astype(BF16)
    lo = (x - hi.astype(F32)).astype(BF16)
    return hi, lo


def _sb_prompt_kernel(bias_ref, q_ref, k_ref, v_ref, tri_ref, o_ref, kb_ref, vt_ref, qt_ref,
                      acc_ref, carry_ref, zbuf_ref):
    p = pl.program_id(1)
    i = pl.program_id(2)
    nblk = kb_ref.shape[0]
    qsub = TQ // TK

    @pl.when(i == 0)
    def _():
        ones_cols = (lax.broadcasted_iota(jnp.int32, (TK, HEAD_PAIR), 1) < BIAS_ROWS).astype(BF16)

        ppb = TK // PAGE_SIZE

        def cast(j, carry):
            kt = jnp.concatenate([k_ref[0, j * ppb + r] for r in range(ppb)], axis=1)
            kb_ref[j, :, 0:HEAD_PAIR] = kt.T.astype(BF16)
            kb_ref[j, :, HEAD_PAIR:2 * HEAD_PAIR] = ones_cols
            vt_ref[j] = jnp.concatenate([v_ref[0, j * ppb + r] for r in range(ppb)], axis=1).astype(BF16)
            return carry
        lax.fori_loop(0, nblk, cast, 0)

    lane = lax.broadcasted_iota(jnp.int32, (1, HEAD_PAIR), 1)
    q = q_ref[0] * (HEAD_DIM ** -0.5 * LOG2E)
    qt_ref[0:HEAD_PAIR, 0:TQ] = jnp.where(lane < HEAD_DIM, q, 0.0).T.astype(BF16)
    qt_ref[0:HEAD_PAIR, TQ:2 * TQ] = jnp.where(lane >= HEAD_DIM, q, 0.0).T.astype(BF16)
    col = lax.broadcasted_iota(jnp.int32, (SUBLANES, 2 * TQ), 1)
    rowi = lax.broadcasted_iota(jnp.int32, (SUBLANES, 2 * TQ), 0)
    rest = jnp.where(col < TQ, bias_ref[2 * p], bias_ref[2 * p + 1]) * LOG2E
    parts = jnp.zeros((SUBLANES, 2 * TQ), F32)
    for r in range(BIAS_ROWS):
        part = rest.astype(BF16).astype(F32)
        parts = jnp.where(rowi == r, part, parts)
        rest = rest - part
    qt_ref[HEAD_PAIR:HEAD_PAIR + SUBLANES, :] = parts.astype(BF16)
    qt_ref[HEAD_PAIR + SUBLANES:, :] = jnp.zeros((HEAD_PAIR - SUBLANES, 2 * TQ), BF16)
    acc_ref[...] = jnp.zeros_like(acc_ref)
    carry_ref[...] = jnp.zeros_like(carry_ref)

    def visit(items, ready=(), ahead=()):
        n = len(items)
        half = TK // 2
        mask = (lax.broadcasted_iota(jnp.int32, (TK, TK), 0) < lax.broadcasted_iota(jnp.int32, (TK, TK), 1))
        cols = [slice(s * TK, (s + 1) * TK) for _, s, _ in items]
        last_of_strip = {s: k for k, (_, s, _) in enumerate(items)}
        zs, lbs, splits, halves = {}, {}, {}, {}
        seen, probs = {}, {}

        def scores(k):
            if k in ready:
                zs[k] = zbuf_ref[k]
            else:
                zs[k] = jnp.dot(kb_ref[items[k][0]], qt_ref[:, cols[k]], preferred_element_type=F32)

        def look_ahead(k):
            if k < len(ahead):
                j, s = ahead[k]
                zbuf_ref[k] = jnp.dot(kb_ref[j], qt_ref[:, s * TK:(s + 1) * TK], preferred_element_type=F32)

        def softplus(k):
            z = zs.pop(k)
            sp = _softplus2(z)
            lbs[k] = z - sp
            spm = jnp.where(mask, sp, 0.0) if items[k][2] else sp
            splits[k] = [jnp.concatenate(_split_bf16(spm[h0:h0 + half]), axis=0) for h0 in (half, 0)]

        def suffix(k):
            out = []
            for hilo in splits.pop(k):
                res = jnp.dot(tri_ref[...], hilo, preferred_element_type=F32)
                out.append((res[0:half], res[half:half + 1]))
            halves[k] = out

        def finish(k):
            j, s, diag = items[k]
            c = cols[k]
            lb = lbs.pop(k)
            (later_new, tot_new), (later_old, tot_old) = halves.pop(k)
            before = carry_ref[:, c] if s not in seen else carry_ref[:, c] + seen[s]
            a_new = jnp.exp2(lb[half:] - later_new - before)
            a_old = jnp.exp2(lb[:half] - later_old - (before + tot_new))
            a = jnp.concatenate([a_old, a_new], axis=0)
            if diag:
                a = jnp.where(mask, a, 0.0)
            probs.setdefault(s, []).append((j, a.astype(BF16)))
            tot = tot_new + tot_old
            seen[s] = tot if s not in seen else seen[s] + tot
            if last_of_strip[s] == k:
                parts = probs.pop(s)
                vblk = jnp.concatenate([vt_ref[jb] for jb, _ in parts], axis=1) if len(parts) > 1 else vt_ref[j]
                ablk = jnp.concatenate([ab for _, ab in parts], axis=0) if len(parts) > 1 else parts[0][1]
                acc_ref[:, c] += jnp.dot(vblk, ablk, preferred_element_type=F32)
                carry_ref[:, c] += seen[s]

        stages = (scores, softplus, look_ahead, suffix, finish)
        for t in range(n + len(stages) - 1):
            for d, stage in enumerate(stages):
                if 0 <= t - d < n:
                    stage(t - d)

    n_old = i * qsub
    strips = range(2 * qsub)
    visit([(i * qsub + kb, hh * qsub + qs, qs == kb)
           for kb in reversed(range(qsub)) for hh in range(2) for qs in range(kb, qsub)],
          ahead=[(jnp.maximum(n_old - 1, 0), s) for s in strips])

    def body(jj, carry):
        j = n_old - 1 - qsub * jj
        visit([(j - r, s, False) for r in range(qsub) for s in strips], ready=tuple(strips),
              ahead=[(jnp.maximum(j - qsub, 0), s) for s in strips])
        return carry

    lax.fori_loop(0, i, body, 0)

    acc = acc_ref[...]
    row = lax.broadcasted_iota(jnp.int32, (HEAD_PAIR, 1), 0)
    out_t = jnp.where(row < HEAD_DIM, acc[:, 0:TQ], acc[:, TQ:2 * TQ])
    o_ref[0] = out_t.T


def _sb_prompt(q, k, v, bias):
    b, s, d = q.shape
    nblk = s // TK
    paged = pl.BlockSpec((1, s // PAGE_SIZE, HEAD_PAIR, PAGE_SIZE), lambda bi, p, i, *_: (bi, 0, p, 0))
    kk = jnp.arange(TK // 2 + SUBLANES)[:, None]
    tri = ((jnp.arange(TK // 2)[None, :] > kk) | (kk == TK // 2)).astype(BF16)
    tri2 = jnp.concatenate([tri, tri], axis=1)
    grid_spec = pltpu.PrefetchScalarGridSpec(
        num_scalar_prefetch=1,
        grid=(b, d // HEAD_PAIR, s // TQ),
        in_specs=[
            pl.BlockSpec((1, TQ, HEAD_PAIR), lambda bi, p, i, *_: (bi, i, p)),
            paged,
            paged,
            pl.BlockSpec((TK // 2 + SUBLANES, TK), lambda bi, p, i, *_: (0, 0)),
        ],
        out_specs=pl.BlockSpec((1, TQ, HEAD_PAIR), lambda bi, p, i, *_: (bi, i, p)),
        scratch_shapes=[
            pltpu.VMEM((nblk, TK, 2 * HEAD_PAIR), BF16),
            pltpu.VMEM((nblk, HEAD_PAIR, TK), BF16),
            pltpu.VMEM((2 * HEAD_PAIR, 2 * TQ), BF16),
            pltpu.VMEM((HEAD_PAIR, 2 * TQ), F32),
            pltpu.VMEM((1, 2 * TQ), F32),
            pltpu.VMEM((2 * TQ // TK, TK, TK), F32),
        ],
    )
    return pl.pallas_call(
        _sb_prompt_kernel,
        grid_spec=grid_spec,
        out_shape=jax.ShapeDtypeStruct((b, s, d), F32),
        compiler_params=_cparams(("arbitrary", "arbitrary", "arbitrary")),
        name="stickbreaking_prompt",
    )(bias, q, k, v, tri2)


def _sb_sample_kernel(n_pages, pt_ref, q_ref, kn_ref, vn_ref, *rest):
    kc_refs = rest[:n_pages]
    vc_refs = rest[n_pages:2 * n_pages]
    bias_ref, tri_ref, o_ref = rest[2 * n_pages:]
    d = q_ref.shape[2]
    rows = N_HEADS * SUBLANES
    contract_lanes = (((1,), (1,)), ((), ()))

    q = q_ref[0] * (HEAD_DIM ** -0.5 * LOG2E)
    qrep = jnp.concatenate([q] * N_HEADS, axis=0)
    head_of_row = lax.broadcasted_iota(jnp.int32, (rows, d), 0) // SUBLANES
    head_of_col = lax.broadcasted_iota(jnp.int32, (rows, d), 1) // HEAD_DIM
    qe = jnp.where(head_of_row == head_of_col, qrep, 0.0).astype(BF16)
    t = lax.broadcasted_iota(jnp.int32, (rows, PAGE_SIZE), 0) % SUBLANES
    mask = lax.broadcasted_iota(jnp.int32, (rows, PAGE_SIZE), 1) < t
    bias = bias_ref[...]
    tri = tri_ref[...]

    n = n_pages + 1
    zs, lbs, hilos, sums, probs = {}, {}, {}, {}, {}
    state = {"before": jnp.zeros((rows, PAGE_SIZE), F32), "acc": None}

    def values_t(k):
        if k == 0:
            pad = jnp.zeros((PAGE_SIZE - SUBLANES, d), F32)
            return jnp.concatenate([vn_ref[0], pad], axis=0).astype(BF16).T
        return vc_refs[n_pages - k][0, 0].astype(BF16)

    def scores(k):
        if k == 0:
            pad = jnp.zeros((PAGE_SIZE - SUBLANES, d), F32)
            kb = jnp.concatenate([kn_ref[0], pad], axis=0).astype(BF16)
            z = lax.dot_general(qe, kb, contract_lanes, preferred_element_type=F32)
        else:
            z = jnp.dot(qe, kc_refs[n_pages - k][0, 0].astype(BF16), preferred_element_type=F32)
        zs[k] = z + bias

    def softplus(k):
        z = zs.pop(k)
        sp = _softplus2(z)
        lbs[k] = z - sp
        spm = jnp.where(mask, sp, 0.0) if k == 0 else sp
        hilos[k] = jnp.concatenate(_split_bf16(spm), axis=1)

    def suffix(k):
        sums[k] = jnp.dot(hilos.pop(k), tri, preferred_element_type=F32)

    def finish(k):
        res = sums.pop(k)
        a = jnp.exp2(lbs.pop(k) - res[:, 0:PAGE_SIZE] - state["before"])
        if k == 0:
            a = jnp.where(mask, a, 0.0)
        state["before"] = state["before"] + res[:, PAGE_SIZE:]
        probs[k] = a.astype(BF16)
        if k % 2 == 1 or k == n - 1:
            ks = sorted(probs)
            ablk = jnp.concatenate([probs.pop(kk) for kk in ks], axis=1) if len(ks) > 1 else probs.pop(ks[0])
            vblk = jnp.concatenate([values_t(kk) for kk in ks], axis=1) if len(ks) > 1 else values_t(ks[0])
            part = lax.dot_general(ablk, vblk, contract_lanes, preferred_element_type=F32)
            state["acc"] = part if state["acc"] is None else state["acc"] + part

    stages = (scores, softplus, suffix, finish)
    for step in range(n + len(stages) - 1):
        for dd, stage in enumerate(stages):
            if 0 <= step - dd < n:
                stage(step - dd)

    acc = state["acc"]
    col_head = lax.broadcasted_iota(jnp.int32, (SUBLANES, d), 1) // HEAD_DIM
    out = jnp.zeros((SUBLANES, d), F32)
    for hh in range(N_HEADS):
        out = out + jnp.where(col_head == hh, acc[hh * SUBLANES:(hh + 1) * SUBLANES, :], 0.0)
    o_ref[0] = out


def _sb_sample(q, k, v, cache_k, cache_v, layer, page_table, bias):
    n, t, d = q.shape
    n_pages = page_table.shape[1]
    rows = N_HEADS * SUBLANES
    bias_rows = jnp.broadcast_to(jnp.repeat(bias * LOG2E, SUBLANES)[:, None], (rows, PAGE_SIZE)).astype(F32)
    kk = jnp.arange(PAGE_SIZE)
    tri = jnp.concatenate([(kk[:, None] > kk[None, :]), jnp.ones((PAGE_SIZE, PAGE_SIZE), bool)], axis=1)
    tri2 = jnp.concatenate([tri, tri], axis=0).astype(BF16)

    def page_spec(r):
        return pl.BlockSpec((1, 1, d, PAGE_SIZE), lambda ni, pt: (layer, pt[ni * n_pages + r], 0, 0))

    seq_spec = pl.BlockSpec((1, t, d), lambda ni, pt: (ni, 0, 0))
    grid_spec = pltpu.PrefetchScalarGridSpec(
        num_scalar_prefetch=1,
        grid=(n,),
        in_specs=[seq_spec, seq_spec, seq_spec]
        + [page_spec(r) for r in range(n_pages)] * 2
        + [pl.BlockSpec((rows, PAGE_SIZE), lambda ni, pt: (0, 0)),
           pl.BlockSpec((2 * PAGE_SIZE, 2 * PAGE_SIZE), lambda ni, pt: (0, 0))],
        out_specs=seq_spec,
    )
    return pl.pallas_call(
        functools.partial(_sb_sample_kernel, n_pages),
        grid_spec=grid_spec,
        out_shape=jax.ShapeDtypeStruct((n, t, d), F32),
        compiler_params=_cparams(("arbitrary",)),
        name="stickbreaking_sample",
    )(page_table.reshape(-1), q, k, v, *([cache_k] * n_pages), *([cache_v] * n_pages), bias_rows, tri2)


def _trunk(sample, x, mod, p, cf_state, sc_state, ffn_state, attend):
    depth = mod.shape[0]
    cf_new, sc_new, ffn_new, k_new, v_new = [], [], [], [], []
    for i in range(depth):
        kind, j = i % 3, i // 3
        if sample:
            m = [jnp.repeat(mod[i, :, c, :], SUBLANES, axis=0) for c in range(6)]
        else:
            m = [mod[i, :, c:c + 1, :] for c in range(6)]
        sh1, sc1, g1, sh2, sc2, g2 = m
        if kind == 0:
            x, nbuf = _cf_layer(sample, x, (sh1, sc1, g1), p['g_pre_mix'][i], p['g_post_mix'][i],
                                p['cf_w1'][j], p['cf_b1'][j], p['cf_w_dw'][j], p['cf_b_dw'][j],
                                p['cf_ln_g'][j], p['cf_ln_b'][j], p['cf_w2'][j], p['cf_b2'][j],
                                None if not sample else cf_state[j])
            cf_new.append(nbuf)
        elif kind == 1:
            x, nbuf = _sc_layer(sample, x, (sh1, sc1, g1), p['g_pre_mix'][i], p['g_post_mix'][i],
                                p['sc_w_in'][j], p['sc_w_conv'][j], p['sc_w_out'][j],
                                None if not sample else sc_state[j])
            sc_new.append(nbuf)
        else:
            q, k, v = _qkv_layer(sample, x, (sh1, sc1), p['g_pre_mix'][i], p['sb_w_qkv'][j])
            o = attend(j, q, k, v)
            x = _attn_out_layer(sample, x, (g1,), p['g_post_mix'][i], p['sb_w_o'][j], o)
            k_new.append(k)
            v_new.append(v)
        x, nbuf = _ffn_layer(sample, x, (sh2, sc2, g2), p['g_pre_ffn'][i], p['g_post_ffn'][i],
                             p['ffn_w_up'][i], p['ffn_w_conv'][i], p['ffn_b_conv'][i], p['ffn_w_down'][i],
                             None if not sample else ffn_state[i])
        ffn_new.append(nbuf)
    return x, cf_new, sc_new, ffn_new, k_new, v_new


def kernel(x_prompt, x_sample, c_prompt, c_sample, state_cf_conv, state_sc_conv, state_ffn_conv, cache_k, cache_v, page_table, g_pre_mix, g_post_mix, g_pre_ffn, g_post_ffn, w_mod, b_mod, ffn_w_up, ffn_w_conv, ffn_b_conv, ffn_w_down, cf_w1, cf_b1, cf_w_dw, cf_b_dw, cf_ln_g, cf_ln_b, cf_w2, cf_b2, sc_w_in, sc_w_conv, sc_w_out, sb_w_qkv, sb_bias, sb_w_o):
    b, s, d = x_prompt.shape
    n, t, _ = x_sample.shape
    depth = w_mod.shape[0]
    row = lambda a: a.reshape(a.shape[0], 1, a.shape[-1])
    p = dict(
        g_pre_mix=row(g_pre_mix), g_post_mix=row(g_post_mix), g_pre_ffn=row(g_pre_ffn), g_post_ffn=row(g_post_ffn),
        ffn_w_up=ffn_w_up.astype(BF16), ffn_w_conv=ffn_w_conv, ffn_b_conv=row(ffn_b_conv),
        ffn_w_down=ffn_w_down.astype(BF16),
        cf_w1=cf_w1.astype(BF16), cf_b1=row(cf_b1), cf_w_dw=cf_w_dw, cf_b_dw=row(cf_b_dw), cf_ln_g=row(cf_ln_g),
        cf_ln_b=row(cf_ln_b), cf_w2=cf_w2.astype(BF16), cf_b2=row(cf_b2),
        sc_w_in=sc_w_in.astype(BF16), sc_w_conv=sc_w_conv, sc_w_out=sc_w_out.astype(BF16),
        sb_w_qkv=sb_w_qkv.astype(BF16), sb_w_o=sb_w_o.astype(BF16))

    rows = b + n
    rows_pad = -(-rows // SUBLANES) * SUBLANES
    c_all = jnp.concatenate([c_prompt, c_sample, jnp.zeros((rows_pad - rows, d), F32)], axis=0)
    mod = _modulation(c_all, w_mod, b_mod).reshape(depth, rows_pad, 6, d)
    mod_p, mod_s = mod[:, :b], mod[:, b:b + n]

    y_p, cf_p, sc_p, ffn_p, k_p, v_p = _trunk(
        False, x_prompt, mod_p, p, None, None, None,
        lambda j, q, k, v: _sb_prompt(q, k, v, sb_bias[j]))

    pool = cache_k.shape[1]
    ck = jnp.transpose(cache_k, (0, 1, 3, 4, 2)).reshape(cache_k.shape[0], pool, d, PAGE_SIZE)
    cv = jnp.transpose(cache_v, (0, 1, 3, 4, 2)).reshape(cache_v.shape[0], pool, d, PAGE_SIZE)

    def attend_sample(j, q, k, v):
        o = _sb_sample(q.reshape(n, t, d), k.reshape(n, t, d), v.reshape(n, t, d), ck, cv, j,
                       page_table, sb_bias[j])
        return o.reshape(n * t, d)

    y_s, cf_s, sc_s, ffn_s, k_s, v_s = _trunk(
        True, x_sample.reshape(n * t, d), mod_s, p, state_cf_conv, state_sc_conv, state_ffn_conv, attend_sample)

    def last(bufs, width):
        return jnp.stack([u[:, u.shape[1] - (width - 1):] for u in bufs])

    def rolled(bufs, state, width):
        outs = []
        for u, st in zip(bufs, state):
            xp = jnp.concatenate([st, u.reshape(n, t, u.shape[-1])], axis=1)
            outs.append(xp[:, xp.shape[1] - (width - 1):])
        return jnp.stack(outs)

    heads = lambda a, lead: jnp.stack(a).reshape((len(a),) + lead + (N_HEADS, HEAD_DIM))

    def pages(a):
        st = jnp.stack(a).reshape(len(a), b, s // PAGE_SIZE, N_HEADS, HEAD_DIM, PAGE_SIZE)
        return jnp.transpose(st, (0, 1, 2, 5, 3, 4))

    return (y_p, y_s.reshape(n, t, d),
            pages(k_p), pages(v_p),
            heads(k_s, (n, t)), heads(v_s, (n, t)),
            last(cf_p, CF_WIDTH), rolled(cf_s, state_cf_conv, CF_WIDTH),
            last(sc_p, SC_WIDTH), rolled(sc_s, state_sc_conv, SC_WIDTH),
            last(ffn_p, FFN_WIDTH), rolled(ffn_s, state_ffn_conv, FFN_WIDTH))
```

```python
import functools
import math

import jax
import jax.numpy as jnp
from jax import lax
from jax.experimental import pallas as pl
from jax.experimental.pallas import tpu as pltpu

F32 = jnp.float32
BF16 = jnp.bfloat16

D_MODEL = 1024
N_HEADS = 16
HEAD_DIM = 64
PAGE_SIZE = 128
RMS_EPS = 1e-6
LN_EPS = 1e-5
LOG2E = 1.4426950408889634

SUBLANES = 8
LANES = 128
VMEM_LIMIT = 56 * 1024 * 1024

TM_PROMPT = 512
TM_SAMPLE = 256
TQ = 512
TK = 256
HEAD_PAIR = 2 * HEAD_DIM


def _cparams(sem):
    return pltpu.CompilerParams(dimension_semantics=sem, vmem_limit_bytes=VMEM_LIMIT)


def _const_spec(shape):
    nd = len(shape)
    return pl.BlockSpec(shape, lambda *_: (0,) * nd, pipeline_mode=pl.Buffered(1))


def _rms(x, g):
    return x * lax.rsqrt(jnp.mean(x * x, axis=-1, keepdims=True) + RMS_EPS) * g


def _ln(x, g, b):
    xc = x - jnp.mean(x, axis=-1, keepdims=True)
    return xc * lax.rsqrt(jnp.mean(xc * xc, axis=-1, keepdims=True) + LN_EPS) * g + b


def _sigmoid(x):
    return 1.0 / (1.0 + jnp.exp(-x))


def _mm(a, w):
    return jnp.dot(a.astype(BF16), w, preferred_element_type=F32)


def _conv_prompt(prev, u, w, width):
    row = lax.broadcasted_iota(jnp.int32, (SUBLANES, 1), 0)
    y = w[width - 1:width, :] * u
    for back in range(1, width):
        rolled = pltpu.roll(u, back, 0)
        head = pltpu.roll(prev, back, 0)
        shifted = jnp.concatenate([jnp.where(row < back, head, rolled[0:SUBLANES]), rolled[SUBLANES:]], axis=0)
        y = y + w[width - 1 - back:width - back, :] * shifted
    return y


def _conv_prompt_wide(ext_ref, u, w_ref, width, first):
    tm = u.shape[0]
    halo = ext_ref.shape[1] - tm

    @pl.when(first)
    def _():
        ext_ref[0, 0:halo, :] = jnp.zeros((halo, ext_ref.shape[2]), F32)

    ext_ref[0, halo:halo + tm, :] = u
    for r in range(1, SUBLANES):
        ext_ref[r, SUBLANES:halo + tm, :] = ext_ref[0, SUBLANES - r:halo + tm - r, :]
    y = None
    for k in range(width):
        back = width - 1 - k
        r = back % SUBLANES
        off = halo - (back - r)
        term = w_ref[k:k + 1, :] * ext_ref[r, off:off + tm, :]
        y = term if y is None else y + term
    tail = ext_ref[0, tm:tm + halo, :]
    ext_ref[0, 0:halo, :] = tail
    return y, tail


def _conv_sample(hist_ref, u, wj_ref, width):
    tb = hist_ref.shape[0]
    c = u.shape[1]
    u3 = u.reshape(tb, SUBLANES, c)
    y = None
    for j in range(width - 1 + SUBLANES):
        if j < width - 1:
            row = hist_ref[:, j:j + 1, :]
        else:
            row = u3[:, j - (width - 1):j - (width - 1) + 1, :]
        term = wj_ref[j][None, :, :] * row
        y = term if y is None else y + term
    return y.reshape(tb * SUBLANES, c)


def _premod(x, g_ref, sh, sc):
    return _rms(x, g_ref[...]) * (1.0 + sc) + sh


def _postmod(x, out, g_ref, gate):
    return x + gate * _rms(out, g_ref[...])


def _modvals(refs, sample):
    return [r[...] if sample else r[0] for r in refs]


def _mod_kernel(c_ref, w_ref, b_ref, o_ref):
    c = c_ref[...]
    act = c * _sigmoid(c)
    o_ref[0] = _mm(act, w_ref[0].astype(BF16)) + b_ref[0]


def _modulation(c_all, w_mod, b_mod):
    depth, d, n = w_mod.shape
    rows = c_all.shape[0]
    tn = 1536
    return pl.pallas_call(
        _mod_kernel,
        grid=(depth, n // tn),
        in_specs=[
            pl.BlockSpec((rows, d), lambda i, j: (0, 0)),
            pl.BlockSpec((1, d, tn), lambda i, j: (i, 0, j)),
            pl.BlockSpec((1, 1, tn), lambda i, j: (i, 0, j)),
        ],
        out_specs=pl.BlockSpec((1, rows, tn), lambda i, j: (i, 0, j)),
        out_shape=jax.ShapeDtypeStruct((depth, rows, n), F32),
        compiler_params=_cparams(("arbitrary", "arbitrary")),
        name="adaln_modulation",
    )(c_all, w_mod, b_mod.reshape(depth, 1, n))


def _row_call(kernel, sample, x, mods, consts, extra_in, outs, scratch, name):
    d = x.shape[-1]
    if sample:
        tm = TM_SAMPLE
        grid = (x.shape[0] // tm,)
        x_spec = pl.BlockSpec((tm, d), lambda i: (i, 0))
        mod_specs = [pl.BlockSpec((tm, d), lambda i: (i, 0)) for _ in mods]
        sem = ("arbitrary",)
    else:
        tm = TM_PROMPT
        grid = (x.shape[0], x.shape[1] // tm)
        x_spec = pl.BlockSpec((1, tm, d), lambda b, i: (b, i, 0))
        mod_specs = [pl.BlockSpec((1, 1, d), lambda b, i: (b, 0, 0)) for _ in mods]
        sem = ("arbitrary", "arbitrary")
    in_specs = [x_spec] + mod_specs + [_const_spec(c.shape) for c in consts] + [s for _, s in extra_in]
    args = [x] + list(mods) + list(consts) + [a for a, _ in extra_in]
    out_shape = [jax.ShapeDtypeStruct(s, dt) for s, dt, _ in outs]
    out_specs = [sp for _, _, sp in outs]
    return pl.pallas_call(
        kernel,
        grid=grid,
        in_specs=in_specs,
        out_specs=out_specs,
        out_shape=out_shape,
        scratch_shapes=scratch,
        compiler_params=_cparams(sem),
        name=name,
    )(*args)


def _x_out(x, sample):
    d = x.shape[-1]
    if sample:
        return (x.shape, F32, pl.BlockSpec((TM_SAMPLE, d), lambda i: (i, 0)))
    return (x.shape, F32, pl.BlockSpec((1, TM_PROMPT, d), lambda b, i: (b, i, 0)))


def _tail_out(batch, halo, c):
    return ((batch, halo, c), F32, pl.BlockSpec((1, halo, c), lambda b, i: (b, 0, 0)))


def _rows_out(rows, c):
    return ((rows, c), F32, pl.BlockSpec((TM_SAMPLE, c), lambda i: (i, 0)))


def _hist_in(state):
    _, w1, c = state.shape
    return (state, pl.BlockSpec((TM_SAMPLE // SUBLANES, w1, c), lambda i: (i, 0, 0)))


def _toeplitz(w):
    width, c = w.shape
    j = jnp.arange(width - 1 + SUBLANES)[:, None]
    t = jnp.arange(SUBLANES)[None, :]
    k = j - t
    valid = (k >= 0) & (k < width)
    return jnp.where(valid[:, :, None], w[jnp.clip(k, 0, width - 1)], 0.0)


CF_WIDTH = 31
CF_HALO = 32


def _cf_kernel(sample, x_ref, sh_ref, sc_ref, gt_ref, gpre_ref, gpost_ref, w1_ref, b1_ref, wdw_ref,
               bdw_ref, lng_ref, lnb_ref, w2_ref, b2_ref, *rest):
    if sample:
        hist_ref, xo_ref, u_ref = rest
    else:
        xo_ref, u_ref, ext_ref = rest
    sh, sc, gt = _modvals((sh_ref, sc_ref, gt_ref), sample)
    x = x_ref[...] if sample else x_ref[0]
    d = x.shape[1]
    h = _premod(x, gpre_ref, sh, sc)
    ag = _mm(h, w1_ref[...]) + b1_ref[...]
    u = ag[:, :d] * _sigmoid(ag[:, d:])
    if sample:
        y = _conv_sample(hist_ref, u, wdw_ref, CF_WIDTH)
        u_ref[...] = u
    else:
        y, tail = _conv_prompt_wide(ext_ref, u, wdw_ref, CF_WIDTH, pl.program_id(1) == 0)
        u_ref[0] = tail
    y = _ln(y + bdw_ref[...], lng_ref[...], lnb_ref[...])
    y = y * _sigmoid(y)
    out = _mm(y, w2_ref[...]) + b2_ref[...]
    xn = _postmod(x, out, gpost_ref, gt)
    if sample:
        xo_ref[...] = xn
    else:
        xo_ref[0] = xn


def _cf_layer(sample, x, mods, gpre, gpost, w1, b1, wdw, bdw, lng, lnb, w2, b2, state=None):
    d = x.shape[-1]
    consts = [gpre, gpost, w1, b1, _toeplitz(wdw) if sample else wdw, bdw, lng, lnb, w2, b2]
    if sample:
        outs = [_x_out(x, True), _rows_out(x.shape[0], d)]
        extra, scratch = [_hist_in(state)], []
    else:
        outs = [_x_out(x, False), _tail_out(x.shape[0], CF_HALO, d)]
        extra, scratch = [], [pltpu.VMEM((SUBLANES, CF_HALO + TM_PROMPT, d), F32)]
    return _row_call(functools.partial(_cf_kernel, sample), sample, x, mods, consts, extra, outs, scratch,
                     "conformer_mixer_sample" if sample else "conformer_mixer_prompt")


SC_WIDTH = 3
SHORT_HALO = 8


def _sc_kernel(sample, x_ref, sh_ref, sc_ref, gt_ref, gpre_ref, gpost_ref, win_ref, wcv_ref, wout_ref, *rest):
    if sample:
        hist_ref, xo_ref, u_ref = rest
    else:
        xo_ref, u_ref, prev_ref = rest
    sh, sc, gt = _modvals((sh_ref, sc_ref, gt_ref), sample)
    x = x_ref[...] if sample else x_ref[0]
    d = x.shape[1]
    h = _premod(x, gpre_ref, sh, sc)
    bcx = _mm(h, win_ref[...])
    cx = bcx[:, d:2 * d] * bcx[:, 2 * d:]
    if sample:
        y = _conv_sample(hist_ref, cx, wcv_ref, SC_WIDTH)
        u_ref[...] = cx
    else:
        @pl.when(pl.program_id(1) == 0)
        def _():
            prev_ref[...] = jnp.zeros_like(prev_ref)

        y = _conv_prompt(prev_ref[...], cx, wcv_ref[...], SC_WIDTH)
        tail = cx[cx.shape[0] - SHORT_HALO:, :]
        prev_ref[...] = tail
        u_ref[0] = tail
    out = _mm(bcx[:, :d] * y, wout_ref[...])
    xn = _postmod(x, out, gpost_ref, gt)
    if sample:
        xo_ref[...] = xn
    else:
        xo_ref[0] = xn


def _sc_layer(sample, x, mods, gpre, gpost, win, wcv, wout, state=None):
    d = x.shape[-1]
    consts = [gpre, gpost, win, _toeplitz(wcv) if sample else wcv, wout]
    if sample:
        outs = [_x_out(x, True), _rows_out(x.shape[0], d)]
        extra, scratch = [_hist_in(state)], []
    else:
        outs = [_x_out(x, False), _tail_out(x.shape[0], SHORT_HALO, d)]
        extra, scratch = [], [pltpu.VMEM((SHORT_HALO, d), F32)]
    return _row_call(functools.partial(_sc_kernel, sample), sample, x, mods, consts, extra, outs, scratch,
                     "shortconv_mixer_sample" if sample else "shortconv_mixer_prompt")


FFN_WIDTH = 3
FFN_CHUNK = 2816


def _ffn_kernel(sample, x_ref, sh_ref, sc_ref, gt_ref, gpre_ref, gpost_ref, wup_ref, wcv_ref, bcv_ref,
                wdn_ref, *rest):
    if sample:
        hist_ref, xo_ref, up_ref = rest
    else:
        xo_ref, up_ref, prev_ref = rest
    sh, sc, gt = _modvals((sh_ref, sc_ref, gt_ref), sample)
    x = x_ref[...] if sample else x_ref[0]
    tm = x.shape[0]
    f = wdn_ref.shape[0]
    h = _premod(x, gpre_ref, sh, sc).astype(BF16)
    out = None
    for c0 in range(0, f, FFN_CHUNK):
        halves = []
        for base in (c0, f + c0):
            cols = slice(base, base + FFN_CHUNK)
            up = jnp.dot(h, wup_ref[:, cols], preferred_element_type=F32)
            if sample:
                tb = tm // SUBLANES
                up3 = up.reshape(tb, SUBLANES, FFN_CHUNK)
                y = None
                for j in range(FFN_WIDTH - 1 + SUBLANES):
                    if j < FFN_WIDTH - 1:
                        row = hist_ref[:, j:j + 1, cols]
                    else:
                        row = up3[:, j - (FFN_WIDTH - 1):j - (FFN_WIDTH - 1) + 1, :]
                    term = wcv_ref[j, :, cols][None, :, :] * row
                    y = term if y is None else y + term
                y = y.reshape(tm, FFN_CHUNK)
                up_ref[:, cols] = up
            else:
                @pl.when(pl.program_id(1) == 0)
                def _():
                    prev_ref[:, cols] = jnp.zeros((SHORT_HALO, FFN_CHUNK), F32)

                y = _conv_prompt(prev_ref[:, cols], up, wcv_ref[:, cols], FFN_WIDTH)
                tail = up[tm - SHORT_HALO:, :]
                prev_ref[:, cols] = tail
                up_ref[0, :, cols] = tail
            halves.append(y + bcv_ref[:, cols])
        a, g = halves
        act = (g * _sigmoid(g)) * a
        part = _mm(act, wdn_ref[c0:c0 + FFN_CHUNK, :])
        out = part if out is None else out + part
    xn = _postmod(x, out, gpost_ref, gt)
    if sample:
        xo_ref[...] = xn
    else:
        xo_ref[0] = xn


def _ffn_layer(sample, x, mods, gpre, gpost, wup, wcv, bcv, wdn, state=None):
    c = wup.shape[1]
    consts = [gpre, gpost, wup, _toeplitz(wcv) if sample else wcv, bcv, wdn]
    if sample:
        outs = [_x_out(x, True), _rows_out(x.shape[0], c)]
        extra, scratch = [_hist_in(state)], []
    else:
        outs = [_x_out(x, False), _tail_out(x.shape[0], SHORT_HALO, c)]
        extra = []
        scratch = [pltpu.VMEM((SHORT_HALO, c), F32)]
    return _row_call(functools.partial(_ffn_kernel, sample), sample, x, mods, consts, extra, outs, scratch,
                     "convffn_sample" if sample else "convffn_prompt")


def _qkv_kernel(sample, x_ref, sh_ref, sc_ref, gpre_ref, w_ref, q_ref, k_ref, v_ref):
    sh, sc = _modvals((sh_ref, sc_ref), sample)
    x = x_ref[...] if sample else x_ref[0]
    d = x.shape[1]
    h = _premod(x, gpre_ref, sh, sc)
    qkv = _mm(h, w_ref[...])
    if sample:
        for idx, ref in enumerate((q_ref, k_ref, v_ref)):
            ref[...] = qkv[:, idx * d:(idx + 1) * d]
    else:
        q_ref[0] = qkv[:, 0:d]
        for idx, ref in ((1, k_ref), (2, v_ref)):
            for pg in range(x.shape[0] // PAGE_SIZE):
                ref[0, pg] = qkv[pg * PAGE_SIZE:(pg + 1) * PAGE_SIZE, idx * d:(idx + 1) * d].T


def _qkv_layer(sample, x, mods, gpre, w):
    if sample:
        outs = [_x_out(x, True)] * 3
    else:
        b, s, d = x.shape
        pages = TM_PROMPT // PAGE_SIZE
        paged = ((b, s // PAGE_SIZE, d, PAGE_SIZE), F32,
                 pl.BlockSpec((1, pages, d, PAGE_SIZE), lambda bi, i: (bi, i, 0, 0)))
        outs = [_x_out(x, False), paged, paged]
    return _row_call(functools.partial(_qkv_kernel, sample), sample, x, mods, [gpre, w], [], outs, [],
                     "qkv_proj_sample" if sample else "qkv_proj_prompt")


def _attn_out_kernel(sample, x_ref, gt_ref, gpost_ref, w_ref, o_ref, xo_ref):
    (gt,) = _modvals((gt_ref,), sample)
    x = x_ref[...] if sample else x_ref[0]
    o = o_ref[...] if sample else o_ref[0]
    xn = _postmod(x, _mm(o, w_ref[...]), gpost_ref, gt)
    if sample:
        xo_ref[...] = xn
    else:
        xo_ref[0] = xn


def _attn_out_layer(sample, x, mods, gpost, w, o):
    _, _, spec = _x_out(x, sample)
    return _row_call(functools.partial(_attn_out_kernel, sample), sample, x, mods, [gpost, w], [(o, spec)],
                     [_x_out(x, sample)], [], "attn_out_sample" if sample else "attn_out_prompt")[0]


SP_CLAMP = 64.0
BIAS_ROWS = 3


def _softplus2(z):
    return jnp.maximum(z, jnp.log2(1.0 + jnp.exp2(jnp.minimum(z, SP_CLAMP))))


def _split_bf16(x):
    hi = x.astype(BF16)
    lo = (x - hi.astype(F32)).astype(BF16)
    return hi, lo


def _sb_prompt_kernel(bias_ref, q_ref, k_ref, v_ref, tri_ref, o_ref, kb_ref, vt_ref, qt_ref,
                      acc_ref, carry_ref, zbuf_ref):
    p = pl.program_id(1)
    i = pl.program_id(2)
    nblk = kb_ref.shape[0]
    qsub = TQ // TK

    @pl.when(i == 0)
    def _():
        ones_cols = (lax.broadcasted_iota(jnp.int32, (TK, HEAD_PAIR), 1) < BIAS_ROWS).astype(BF16)

        ppb = TK // PAGE_SIZE

        def cast(j, carry):
            kt = jnp.concatenate([k_ref[0, j * ppb + r] for r in range(ppb)], axis=1)
            kb_ref[j, :, 0:HEAD_PAIR] = kt.T.astype(BF16)
            kb_ref[j, :, HEAD_PAIR:2 * HEAD_PAIR] = ones_cols
            vt_ref[j] = jnp.concatenate([v_ref[0, j * ppb + r] for r in range(ppb)], axis=1).astype(BF16)
            return carry
        lax.fori_loop(0, nblk, cast, 0)

    lane = lax.broadcasted_iota(jnp.int32, (1, HEAD_PAIR), 1)
    q = q_ref[0] * (HEAD_DIM ** -0.5 * LOG2E)
    qt_ref[0:HEAD_PAIR, 0:TQ] = jnp.where(lane < HEAD_DIM, q, 0.0).T.astype(BF16)
    qt_ref[0:HEAD_PAIR, TQ:2 * TQ] = jnp.where(lane >= HEAD_DIM, q, 0.0).T.astype(BF16)
    col = lax.broadcasted_iota(jnp.int32, (SUBLANES, 2 * TQ), 1)
    rowi = lax.broadcasted_iota(jnp.int32, (SUBLANES, 2 * TQ), 0)
    rest = jnp.where(col < TQ, bias_ref[2 * p], bias_ref[2 * p + 1]) * LOG2E
    parts = jnp.zeros((SUBLANES, 2 * TQ), F32)
    for r in range(BIAS_ROWS):
        part = rest.astype(BF16).astype(F32)
        parts = jnp.where(rowi == r, part, parts)
        rest = rest - part
    qt_ref[HEAD_PAIR:HEAD_PAIR + SUBLANES, :] = parts.astype(BF16)
    qt_ref[HEAD_PAIR + SUBLANES:, :] = jnp.zeros((HEAD_PAIR - SUBLANES, 2 * TQ), BF16)
    acc_ref[...] = jnp.zeros_like(acc_ref)
    carry_ref[...] = jnp.zeros_like(carry_ref)

    def visit(items, ready=(), ahead=()):
        n = len(items)
        half = TK // 2
        mask = (lax.broadcasted_iota(jnp.int32, (TK, TK), 0) < lax.broadcasted_iota(jnp.int32, (TK, TK), 1))
        cols = [slice(s * TK, (s + 1) * TK) for _, s, _ in items]
        last_of_strip = {s: k for k, (_, s, _) in enumerate(items)}
        zs, lbs, splits, halves = {}, {}, {}, {}
        seen, probs = {}, {}

        def scores(k):
            if k in ready:
                zs[k] = zbuf_ref[k]
            else:
                zs[k] = jnp.dot(kb_ref[items[k][0]], qt_ref[:, cols[k]], preferred_element_type=F32)

        def look_ahead(k):
            if k < len(ahead):
                j, s = ahead[k]
                zbuf_ref[k] = jnp.dot(kb_ref[j], qt_ref[:, s * TK:(s + 1) * TK], preferred_element_type=F32)

        def softplus(k):
            z = zs.pop(k)
            sp = _softplus2(z)
            lbs[k] = z - sp
            spm = jnp.where(mask, sp, 0.0) if items[k][2] else sp
            splits[k] = [jnp.concatenate(_split_bf16(spm[h0:h0 + half]), axis=0) for h0 in (half, 0)]

        def suffix(k):
            out = []
            for hilo in splits.pop(k):
                res = jnp.dot(tri_ref[...], hilo, preferred_element_type=F32)
                out.append((res[0:half], res[half:half + 1]))
            halves[k] = out

        def finish(k):
            j, s, diag = items[k]
            c = cols[k]
            lb = lbs.pop(k)
            (later_new, tot_new), (later_old, tot_old) = halves.pop(k)
            before = carry_ref[:, c] if s not in seen else carry_ref[:, c] + seen[s]
            a_new = jnp.exp2(lb[half:] - later_new - before)
            a_old = jnp.exp2(lb[:half] - later_old - (before + tot_new))
            a = jnp.concatenate([a_old, a_new], axis=0)
            if diag:
                a = jnp.where(mask, a, 0.0)
            probs.setdefault(s, []).append((j, a.astype(BF16)))
            tot = tot_new + tot_old
            seen[s] = tot if s not in seen else seen[s] + tot
            if last_of_strip[s] == k:
                parts = probs.pop(s)
                vblk = jnp.concatenate([vt_ref[jb] for jb, _ in parts], axis=1) if len(parts) > 1 else vt_ref[j]
                ablk = jnp.concatenate([ab for _, ab in parts], axis=0) if len(parts) > 1 else parts[0][1]
                acc_ref[:, c] += jnp.dot(vblk, ablk, preferred_element_type=F32)
                carry_ref[:, c] += seen[s]

        stages = (scores, softplus, look_ahead, suffix, finish)
        for t in range(n + len(stages) - 1):
            for d, stage in enumerate(stages):
                if 0 <= t - d < n:
                    stage(t - d)

    n_old = i * qsub
    strips = range(2 * qsub)
    visit([(i * qsub + kb, hh * qsub + qs, qs == kb)
           for kb in reversed(range(qsub)) for hh in range(2) for qs in range(kb, qsub)],
          ahead=[(jnp.maximum(n_old - 1, 0), s) for s in strips])

    def body(jj, carry):
        j = n_old - 1 - qsub * jj
        visit([(j - r, s, False) for r in range(qsub) for s in strips], ready=tuple(strips),
              ahead=[(jnp.maximum(j - qsub, 0), s) for s in strips])
        return carry

    lax.fori_loop(0, i, body, 0)

    acc = acc_ref[...]
    row = lax.broadcasted_iota(jnp.int32, (HEAD_PAIR, 1), 0)
    out_t = jnp.where(row < HEAD_DIM, acc[:, 0:TQ], acc[:, TQ:2 * TQ])
    o_ref[0] = out_t.T


def _sb_prompt(q, k, v, bias):
    b, s, d = q.shape
    nblk = s // TK
    paged = pl.BlockSpec((1, s // PAGE_SIZE, HEAD_PAIR, PAGE_SIZE), lambda bi, p, i, *_: (bi, 0, p, 0))
    kk = jnp.arange(TK // 2 + SUBLANES)[:, None]
    tri = ((jnp.arange(TK // 2)[None, :] > kk) | (kk == TK // 2)).astype(BF16)
    tri2 = jnp.concatenate([tri, tri], axis=1)
    grid_spec = pltpu.PrefetchScalarGridSpec(
        num_scalar_prefetch=1,
        grid=(b, d // HEAD_PAIR, s // TQ),
        in_specs=[
            pl.BlockSpec((1, TQ, HEAD_PAIR), lambda bi, p, i, *_: (bi, i, p)),
            paged,
            paged,
            pl.BlockSpec((TK // 2 + SUBLANES, TK), lambda bi, p, i, *_: (0, 0)),
        ],
        out_specs=pl.BlockSpec((1, TQ, HEAD_PAIR), lambda bi, p, i, *_: (bi, i, p)),
        scratch_shapes=[
            pltpu.VMEM((nblk, TK, 2 * HEAD_PAIR), BF16),
            pltpu.VMEM((nblk, HEAD_PAIR, TK), BF16),
            pltpu.VMEM((2 * HEAD_PAIR, 2 * TQ), BF16),
            pltpu.VMEM((HEAD_PAIR, 2 * TQ), F32),
            pltpu.VMEM((1, 2 * TQ), F32),
            pltpu.VMEM((2 * TQ // TK, TK, TK), F32),
        ],
    )
    return pl.pallas_call(
        _sb_prompt_kernel,
        grid_spec=grid_spec,
        out_shape=jax.ShapeDtypeStruct((b, s, d), F32),
        compiler_params=_cparams(("arbitrary", "arbitrary", "arbitrary")),
        name="stickbreaking_prompt",
    )(bias, q, k, v, tri2)


def _sb_sample_kernel(n_pages, pt_ref, q_ref, kn_ref, vn_ref, *rest):
    kc_refs = rest[:n_pages]
    vc_refs = rest[n_pages:2 * n_pages]
    bias_ref, tri_ref, o_ref = rest[2 * n_pages:]
    d = q_ref.shape[2]
    rows = N_HEADS * SUBLANES
    contract_lanes = (((1,), (1,)), ((), ()))

    q = q_ref[0] * (HEAD_DIM ** -0.5 * LOG2E)
    qrep = jnp.concatenate([q] * N_HEADS, axis=0)
    head_of_row = lax.broadcasted_iota(jnp.int32, (rows, d), 0) // SUBLANES
    head_of_col = lax.broadcasted_iota(jnp.int32, (rows, d), 1) // HEAD_DIM
    qe = jnp.where(head_of_row == head_of_col, qrep, 0.0).astype(BF16)
    t = lax.broadcasted_iota(jnp.int32, (rows, PAGE_SIZE), 0) % SUBLANES
    mask = lax.broadcasted_iota(jnp.int32, (rows, PAGE_SIZE), 1) < t
    bias = bias_ref[...]
    tri = tri_ref[...]

    n = n_pages + 1
    zs, lbs, hilos, sums, probs = {}, {}, {}, {}, {}
    state = {"before": jnp.zeros((rows, PAGE_SIZE), F32), "acc": None}

    def values_t(k):
        if k == 0:
            pad = jnp.zeros((PAGE_SIZE - SUBLANES, d), F32)
            return jnp.concatenate([vn_ref[0], pad], axis=0).astype(BF16).T
        return vc_refs[n_pages - k][0, 0].astype(BF16)

    def scores(k):
        if k == 0:
            pad = jnp.zeros((PAGE_SIZE - SUBLANES, d), F32)
            kb = jnp.concatenate([kn_ref[0], pad], axis=0).astype(BF16)
            z = lax.dot_general(qe, kb, contract_lanes, preferred_element_type=F32)
        else:
            z = jnp.dot(qe, kc_refs[n_pages - k][0, 0].astype(BF16), preferred_element_type=F32)
        zs[k] = z + bias

    def softplus(k):
        z = zs.pop(k)
        sp = _softplus2(z)
        lbs[k] = z - sp
        spm = jnp.where(mask, sp, 0.0) if k == 0 else sp
        hilos[k] = jnp.concatenate(_split_bf16(spm), axis=1)

    def suffix(k):
        sums[k] = jnp.dot(hilos.pop(k), tri, preferred_element_type=F32)

    def finish(k):
        res = sums.pop(k)
        a = jnp.exp2(lbs.pop(k) - res[:, 0:PAGE_SIZE] - state["before"])
        if k == 0:
            a = jnp.where(mask, a, 0.0)
        state["before"] = state["before"] + res[:, PAGE_SIZE:]
        probs[k] = a.astype(BF16)
        if k % 2 == 1 or k == n - 1:
            ks = sorted(probs)
            ablk = jnp.concatenate([probs.pop(kk) for kk in ks], axis=1) if len(ks) > 1 else probs.pop(ks[0])
            vblk = jnp.concatenate([values_t(kk) for kk in ks], axis=1) if len(ks) > 1 else values_t(ks[0])
            part = lax.dot_general(ablk, vblk, contract_lanes, preferred_element_type=F32)
            state["acc"] = part if state["acc"] is None else state["acc"] + part

    stages = (scores, softplus, suffix, finish)
    for step in range(n + len(stages) - 1):
        for dd, stage in enumerate(stages):
            if 0 <= step - dd < n:
                stage(step - dd)

    acc = state["acc"]
    col_head = lax.broadcasted_iota(jnp.int32, (SUBLANES, d), 1) // HEAD_DIM
    out = jnp.zeros((SUBLANES, d), F32)
    for hh in range(N_HEADS):
        out = out + jnp.where(col_head == hh, acc[hh * SUBLANES:(hh + 1) * SUBLANES, :], 0.0)
    o_ref[0] = out


def _sb_sample(q, k, v, cache_k, cache_v, layer, page_table, bias):
    n, t, d = q.shape
    n_pages = page_table.shape[1]
    rows = N_HEADS * SUBLANES
    bias_rows = jnp.broadcast_to(jnp.repeat(bias * LOG2E, SUBLANES)[:, None], (rows, PAGE_SIZE)).astype(F32)
    kk = jnp.arange(PAGE_SIZE)
    tri = jnp.concatenate([(kk[:, None] > kk[None, :]), jnp.ones((PAGE_SIZE, PAGE_SIZE), bool)], axis=1)
    tri2 = jnp.concatenate([tri, tri], axis=0).astype(BF16)

    def page_spec(r):
        return pl.BlockSpec((1, 1, d, PAGE_SIZE), lambda ni, pt: (layer, pt[ni * n_pages + r], 0, 0))

    seq_spec = pl.BlockSpec((1, t, d), lambda ni, pt: (ni, 0, 0))
    grid_spec = pltpu.PrefetchScalarGridSpec(
        num_scalar_prefetch=1,
        grid=(n,),
        in_specs=[seq_spec, seq_spec, seq_spec]
        + [page_spec(r) for r in range(n_pages)] * 2
        + [pl.BlockSpec((rows, PAGE_SIZE), lambda ni, pt: (0, 0)),
           pl.BlockSpec((2 * PAGE_SIZE, 2 * PAGE_SIZE), lambda ni, pt: (0, 0))],
        out_specs=seq_spec,
    )
    return pl.pallas_call(
        functools.partial(_sb_sample_kernel, n_pages),
        grid_spec=grid_spec,
        out_shape=jax.ShapeDtypeStruct((n, t, d), F32),
        compiler_params=_cparams(("arbitrary",)),
        name="stickbreaking_sample",
    )(page_table.reshape(-1), q, k, v, *([cache_k] * n_pages), *([cache_v] * n_pages), bias_rows, tri2)


def _trunk(sample, x, mod, p, cf_state, sc_state, ffn_state, attend):
    depth = mod.shape[0]
    cf_new, sc_new, ffn_new, k_new, v_new = [], [], [], [], []
    for i in range(depth):
        kind, j = i % 3, i // 3
        if sample:
            m = [jnp.repeat(mod[i, :, c, :], SUBLANES, axis=0) for c in range(6)]
        else:
            m = [mod[i, :, c:c + 1, :] for c in range(6)]
        sh1, sc1, g1, sh2, sc2, g2 = m
        if kind == 0:
            x, nbuf = _cf_layer(sample, x, (sh1, sc1, g1), p['g_pre_mix'][i], p['g_post_mix'][i],
                                p['cf_w1'][j], p['cf_b1'][j], p['cf_w_dw'][j], p['cf_b_dw'][j],
                                p['cf_ln_g'][j], p['cf_ln_b'][j], p['cf_w2'][j], p['cf_b2'][j],
                                None if not sample else cf_state[j])
            cf_new.append(nbuf)
        elif kind == 1:
            x, nbuf = _sc_layer(sample, x, (sh1, sc1, g1), p['g_pre_mix'][i], p['g_post_mix'][i],
                                p['sc_w_in'][j], p['sc_w_conv'][j], p['sc_w_out'][j],
                                None if not sample else sc_state[j])
            sc_new.append(nbuf)
        else:
            q, k, v = _qkv_layer(sample, x, (sh1, sc1), p['g_pre_mix'][i], p['sb_w_qkv'][j])
            o = attend(j, q, k, v)
            x = _attn_out_layer(sample, x, (g1,), p['g_post_mix'][i], p['sb_w_o'][j], o)
            k_new.append(k)
            v_new.append(v)
        x, nbuf = _ffn_layer(sample, x, (sh2, sc2, g2), p['g_pre_ffn'][i], p['g_post_ffn'][i],
                             p['ffn_w_up'][i], p['ffn_w_conv'][i], p['ffn_b_conv'][i], p['ffn_w_down'][i],
                             None if not sample else ffn_state[i])
        ffn_new.append(nbuf)
    return x, cf_new, sc_new, ffn_new, k_new, v_new


def kernel(x_prompt, x_sample, c_prompt, c_sample, state_cf_conv, state_sc_conv, state_ffn_conv, cache_k, cache_v, page_table, g_pre_mix, g_post_mix, g_pre_ffn, g_post_ffn, w_mod, b_mod, ffn_w_up, ffn_w_conv, ffn_b_conv, ffn_w_down, cf_w1, cf_b1, cf_w_dw, cf_b_dw, cf_ln_g, cf_ln_b, cf_w2, cf_b2, sc_w_in, sc_w_conv, sc_w_out, sb_w_qkv, sb_bias, sb_w_o):
    b, s, d = x_prompt.shape
    n, t, _ = x_sample.shape
    depth = w_mod.shape[0]
    row = lambda a: a.reshape(a.shape[0], 1, a.shape[-1])
    p = dict(
        g_pre_mix=row(g_pre_mix), g_post_mix=row(g_post_mix), g_pre_ffn=row(g_pre_ffn), g_post_ffn=row(g_post_ffn),
        ffn_w_up=ffn_w_up.astype(BF16), ffn_w_conv=ffn_w_conv, ffn_b_conv=row(ffn_b_conv),
        ffn_w_down=ffn_w_down.astype(BF16),
        cf_w1=cf_w1.astype(BF16), cf_b1=row(cf_b1), cf_w_dw=cf_w_dw, cf_b_dw=row(cf_b_dw), cf_ln_g=row(cf_ln_g),
        cf_ln_b=row(cf_ln_b), cf_w2=cf_w2.astype(BF16), cf_b2=row(cf_b2),
        sc_w_in=sc_w_in.astype(BF16), sc_w_conv=sc_w_conv, sc_w_out=sc_w_out.astype(BF16),
        sb_w_qkv=sb_w_qkv.astype(BF16), sb_w_o=sb_w_o.astype(BF16))

    rows = b + n
    rows_pad = -(-rows // SUBLANES) * SUBLANES
    c_all = jnp.concatenate([c_prompt, c_sample, jnp.zeros((rows_pad - rows, d), F32)], axis=0)
    mod = _modulation(c_all, w_mod, b_mod).reshape(depth, rows_pad, 6, d)
    mod_p, mod_s = mod[:, :b], mod[:, b:b + n]

    y_p, cf_p, sc_p, ffn_p, k_p, v_p = _trunk(
        False, x_prompt, mod_p, p, None, None, None,
        lambda j, q, k, v: _sb_prompt(q, k, v, sb_bias[j]))

    pool = cache_k.shape[1]
    ck = jnp.transpose(cache_k, (0, 1, 3, 4, 2)).reshape(cache_k.shape[0], pool, d, PAGE_SIZE)
    cv = jnp.transpose(cache_v, (0, 1, 3, 4, 2)).reshape(cache_v.shape[0], pool, d, PAGE_SIZE)

    def attend_sample(j, q, k, v):
        o = _sb_sample(q.reshape(n, t, d), k.reshape(n, t, d), v.reshape(n, t, d), ck, cv, j,
                       page_table, sb_bias[j])
        return o.reshape(n * t, d)

    y_s, cf_s, sc_s, ffn_s, k_s, v_s = _trunk(
        True, x_sample.reshape(n * t, d), mod_s, p, state_cf_conv, state_sc_conv, state_ffn_conv, attend_sample)

    def last(bufs, width):
        return jnp.stack([u[:, u.shape[1] - (width - 1):] for u in bufs])

    def rolled(bufs, state, width):
        outs = []
        for u, st in zip(bufs, state):
            xp = jnp.concatenate([st, u.reshape(n, t, u.shape[-1])], axis=1)
            outs.append(xp[:, xp.shape[1] - (width - 1):])
        return jnp.stack(outs)

    heads = lambda a, lead: jnp.stack(a).reshape((len(a),) + lead + (N_HEADS, HEAD_DIM))

    def pages(a):
        st = jnp.stack(a).reshape(len(a), b, s // PAGE_SIZE, N_HEADS, HEAD_DIM, PAGE_SIZE)
        return jnp.transpose(st, (0, 1, 2, 5, 3, 4))

    return (y_p, y_s.reshape(n, t, d),
            pages(k_p), pages(v_p),
            heads(k_s, (n, t)), heads(v_s, (n, t)),
            last(cf_p, CF_WIDTH), rolled(cf_s, state_cf_conv, CF_WIDTH),
            last(sc_p, SC_WIDTH), rolled(sc_s, state_sc_conv, SC_WIDTH),
            last(ffn_p, FFN_WIDTH), rolled(ffn_s, state_ffn_conv, FFN_WIDTH))
```
